```python
import math
import jax, jax.numpy as jnp
from jax import lax
import numpy as np

D_MODEL = 1024
BATCH = 8
SEQ = 4096
DEPTH = 2

N_EVEN = (DEPTH + 1) // 2
N_ODD = DEPTH // 2
N_SUB = 3
RMS_EPS = 1e-6
FFN_RES = 0.5
D_FF = ((8 * D_MODEL // 3 + 127) // 128) * 128
POOL_WINDOWS = (2, 4, 8, 16)
POOL_GROUPS = len(POOL_WINDOWS)
POOL_DIM = D_MODEL // 2
POOL_GROUP_DIM = POOL_DIM // POOL_GROUPS
D_SSM = D_MODEL
SSD_HEADDIM = 64
SSD_HEADS = D_SSM // SSD_HEADDIM
SSD_GROUPS = 2
D_STATE = 128
CONV_K = 4
CONV_DIM = D_SSM + 2 * SSD_GROUPS * D_STATE
SSD_CHUNK = 128
IN_AB = POOL_DIM + D_SSM + CONV_DIM + SSD_HEADS
OUT_AB = POOL_DIM + D_SSM
MLA_HEADS = 16
QK_NOPE = 64
QK_ROPE = 32
V_DIM = 64
Q_LORA = 768
KV_LORA = 256
IN_MLA = Q_LORA + KV_LORA + QK_ROPE
ROPE_THETA = 10000.0
Q_BLOCK = 128

kernel_name = "hybrid_pool_ssd_mla_macaron_adaln"


def rmsnorm(x, g):
    xf = x.astype(jnp.float32)
    y = xf * lax.rsqrt(jnp.mean(xf * xf, axis=-1, keepdims=True) + RMS_EPS)
    return (y * g.astype(jnp.float32)).astype(x.dtype)


def modulate(h, shift, scale):
    return h * (1 + scale[:, None, :]) + shift[:, None, :]


def swiglu(h, w13, w2):
    a, b = jnp.split(h @ w13, 2, axis=-1)
    return (jax.nn.silu(a) * b) @ w2


def multiscale_pool(u, w, scale):
    bsz, L, _ = u.shape
    uf = u.astype(jnp.float32)
    cs_pad = jnp.pad(jnp.cumsum(uf, axis=1), ((0, 0), (1, 0), (0, 0)))
    t = jnp.arange(L, dtype=jnp.float32)
    outs = []
    for gi, win in enumerate(POOL_WINDOWS):
        sl = slice(gi * POOL_GROUP_DIM, (gi + 1) * POOL_GROUP_DIM)
        c_g = cs_pad[:, :, sl]
        lag = jnp.pad(c_g[:, :L + 1 - win], ((0, 0), (win - 1, 0), (0, 0)))
        count = jnp.minimum(t + 1.0, float(win))[None, :, None]
        outs.append((c_g[:, 1:] - lag) / count - uf[:, :, sl])
    d = jnp.stack(outs, axis=2)
    y = jnp.einsum("blgc,gcd->blgd", d, w.astype(jnp.float32)).reshape(bsz, L, POOL_DIM)
    return (y * scale.astype(jnp.float32)).astype(u.dtype)


def causal_depthwise_conv(u, w, b):
    y = lax.conv_general_dilated(u, w[:, None, :], window_strides=(1,),
                                 padding=[(CONV_K - 1, 0)],
                                 dimension_numbers=("NWC", "WIO", "NWC"),
                                 feature_group_count=u.shape[-1])
    return y + b


def ssd_chunked(xh, dt, a, bm, cm):
    bsz, L, H, P = xh.shape
    nc = L // SSD_CHUNK
    r = H // SSD_GROUPS
    x = (xh * dt[..., None]).reshape(bsz, nc, SSD_CHUNK, SSD_GROUPS, r, P)
    adt = (dt * a).reshape(bsz, nc, SSD_CHUNK, SSD_GROUPS, r).transpose(0, 3, 4, 1, 2)
    bc = bm.reshape(bsz, nc, SSD_CHUNK, SSD_GROUPS, D_STATE)
    cc = cm.reshape(bsz, nc, SSD_CHUNK, SSD_GROUPS, D_STATE)
    a_cs = jnp.cumsum(adt, axis=-1)
    causal = jnp.tril(jnp.ones((SSD_CHUNK, SSD_CHUNK), dtype=bool))
    decay = jnp.exp(jnp.where(causal, a_cs[..., :, None] - a_cs[..., None, :], -jnp.inf))
    cb = jnp.einsum("bclgn,bcsgn->bgcls", cc, bc)
    y_diag = jnp.einsum("bgcls,bgrcls,bcsgrp->bclgrp", cb, decay, x)
    decay_to_end = jnp.exp(a_cs[..., -1:] - a_cs)
    states = jnp.einsum("bcsgn,bgrcs,bcsgrp->cbgrpn", bc, decay_to_end, x)
    chunk_decay = jnp.exp(a_cs[..., -1]).transpose(3, 0, 1, 2)

    def step(h, inp):
        s_c, d_c = inp
        return d_c[..., None, None] * h + s_c, h

    h0 = jnp.zeros(states.shape[1:], states.dtype)
    _, prev = lax.scan(step, h0, (states, chunk_decay))
    y_off = jnp.einsum("bclgn,cbgrpn,bgrcl->bclgrp", cc, prev, jnp.exp(a_cs))
    return (y_diag + y_off).reshape(bsz, L, H, P)


def pool_ssd_mixer(h, w_in, pool_w, pool_scale, conv_w, conv_b, dt_bias, a_log, d_skip, norm_g, w_out):
    bsz, L, _ = h.shape
    proj = h @ w_in
    u_pool, z, xbc, dt_raw = jnp.split(
        proj, [POOL_DIM, POOL_DIM + D_SSM, POOL_DIM + D_SSM + CONV_DIM], axis=-1)
    y_pool = multiscale_pool(u_pool, pool_w, pool_scale)
    xbc = jax.nn.silu(causal_depthwise_conv(xbc, conv_w, conv_b))
    xs, bm, cm = jnp.split(xbc, [D_SSM, D_SSM + SSD_GROUPS * D_STATE], axis=-1)
    f32 = jnp.float32
    xh = xs.astype(f32).reshape(bsz, L, SSD_HEADS, SSD_HEADDIM)
    dt = jax.nn.softplus(dt_raw.astype(f32) + dt_bias.astype(f32))
    a = -jnp.exp(a_log.astype(f32))
    y = ssd_chunked(xh, dt, a,
                    bm.astype(f32).reshape(bsz, L, SSD_GROUPS, D_STATE),
                    cm.astype(f32).reshape(bsz, L, SSD_GROUPS, D_STATE))
    y = y + d_skip.astype(f32)[:, None] * xh
    y = y.reshape(bsz, L, D_SSM) * jax.nn.silu(z.astype(f32))
    yg = y.reshape(bsz, L, SSD_GROUPS, D_SSM // SSD_GROUPS)
    yg = yg * lax.rsqrt(jnp.mean(yg * yg, axis=-1, keepdims=True) + RMS_EPS)
    y = (yg.reshape(bsz, L, D_SSM) * norm_g.astype(f32)).astype(h.dtype)
    return jnp.concatenate([y_pool, y], axis=-1) @ w_out


def rope_tables(positions):
    inv_freq = ROPE_THETA ** (-jnp.arange(0, QK_ROPE, 2, dtype=jnp.float32) / QK_ROPE)
    ang = positions.astype(jnp.float32)[..., None] * inv_freq
    return jnp.cos(ang), jnp.sin(ang)


def apply_rope(x, cos, sin):
    xf = x.astype(jnp.float32)
    x1, x2 = jnp.split(xf, 2, axis=-1)
    return jnp.concatenate([x1 * cos - x2 * sin, x2 * cos + x1 * sin], axis=-1).astype(x.dtype)


def mla_mixer(h, positions, w_in, q_norm_g, w_uq, kv_norm_g, w_ukv, w_o):
    bsz, L, _ = h.shape
    q_a, kv_a, k_rope = jnp.split(h @ w_in, [Q_LORA, Q_LORA + KV_LORA], axis=-1)
    q = (rmsnorm(q_a, q_norm_g) @ w_uq).reshape(bsz, L, MLA_HEADS, QK_NOPE + QK_ROPE)
    q_nope, q_rope = q[..., :QK_NOPE], q[..., QK_NOPE:]
    kv = (rmsnorm(kv_a, kv_norm_g) @ w_ukv).reshape(bsz, L, MLA_HEADS, QK_NOPE + V_DIM)
    k_nope, v = kv[..., :QK_NOPE], kv[..., QK_NOPE:]
    cos, sin = rope_tables(positions)
    q_rope = apply_rope(q_rope, cos[:, :, None], sin[:, :, None])
    k_rope = apply_rope(k_rope, cos, sin)
    scale = 1.0 / math.sqrt(QK_NOPE + QK_ROPE)
    nb = L // Q_BLOCK
    k_idx = jnp.arange(L)

    def blocks(t):
        return jnp.moveaxis(t.reshape(bsz, nb, Q_BLOCK, *t.shape[2:]), 1, 0)

    def attend(args):
        qn, qr, start = args
        s = (jnp.einsum("bqhd,bkhd->bhqk", qn, k_nope, preferred_element_type=jnp.float32)
             + jnp.einsum("bqhd,bkd->bhqk", qr, k_rope, preferred_element_type=jnp.float32)) * scale
        q_idx = start + jnp.arange(Q_BLOCK)
        s = jnp.where(k_idx[None, :] <= q_idx[:, None], s, -jnp.inf)
        p = jax.nn.softmax(s, axis=-1).astype(v.dtype)
        return jnp.einsum("bhqk,bkhd->bqhd", p, v)

    starts = jnp.arange(nb, dtype=jnp.int32) * Q_BLOCK
    o = lax.map(attend, (blocks(q_nope), blocks(q_rope), starts))
    o = jnp.moveaxis(o, 0, 1).reshape(bsz, L, MLA_HEADS * V_DIM)
    return o @ w_o


def setup_inputs(seed: int = 0) -> dict:
    key = jax.random.key(seed)
    ks = jax.random.split(key, 32)
    f32 = jnp.float32

    def nrm(k, shape, std):
        return jax.random.normal(k, shape, f32) * std

    def gain(k, shape):
        return 1.0 + 0.05 * jax.random.normal(k, shape, f32)

    x = jax.random.normal(ks[0], (BATCH, SEQ, D_MODEL), f32)
    c = jax.random.normal(ks[1], (BATCH, D_MODEL), f32)
    positions = (jnp.arange(SEQ, dtype=jnp.int32)[None, :]
                 + jax.random.randint(ks[2], (BATCH, 1), 0, 1024, dtype=jnp.int32))
    mod_w = nrm(ks[3], (DEPTH, D_MODEL, 3 * N_SUB * D_MODEL), 0.5 * D_MODEL ** -0.5)
    mod_b = nrm(ks[4], (DEPTH, 3 * N_SUB * D_MODEL), 0.02)
    norm_g = gain(ks[5], (DEPTH, N_SUB, D_MODEL))
    ffn_w13 = nrm(ks[6], (DEPTH, 2, D_MODEL, 2 * D_FF), D_MODEL ** -0.5)
    ffn_w2 = nrm(ks[7], (DEPTH, 2, D_FF, D_MODEL), D_FF ** -0.5)
    ab_w_in = nrm(ks[8], (N_EVEN, D_MODEL, IN_AB), D_MODEL ** -0.5)
    pool_w = nrm(ks[9], (N_EVEN, POOL_GROUPS, POOL_GROUP_DIM, POOL_GROUP_DIM), POOL_GROUP_DIM ** -0.5)
    pool_scale = gain(ks[10], (N_EVEN, POOL_DIM))
    ssd_conv_w = nrm(ks[11], (N_EVEN, CONV_K, CONV_DIM), CONV_K ** -0.5)
    ssd_conv_b = nrm(ks[12], (N_EVEN, CONV_DIM), 0.02)
    dt0 = jnp.exp(jax.random.uniform(ks[13], (N_EVEN, SSD_HEADS), f32,
                                     minval=math.log(1e-3), maxval=math.log(1e-1)))
    ssd_dt_bias = dt0 + jnp.log(-jnp.expm1(-dt0))
    ssd_a_log = jnp.log(jax.random.uniform(ks[14], (N_EVEN, SSD_HEADS), f32, minval=1.0, maxval=16.0))
    ssd_d = gain(ks[15], (N_EVEN, SSD_HEADS))
    ssd_norm_g = gain(ks[16], (N_EVEN, D_SSM))
    ab_w_out = nrm(ks[17], (N_EVEN, OUT_AB, D_MODEL), OUT_AB ** -0.5)
    mla_w_in = nrm(ks[18], (N_ODD, D_MODEL, IN_MLA), D_MODEL ** -0.5)
    mla_q_norm_g = gain(ks[19], (N_ODD, Q_LORA))
    mla_w_uq = nrm(ks[20], (N_ODD, Q_LORA, MLA_HEADS * (QK_NOPE + QK_ROPE)), Q_LORA ** -0.5)
    mla_kv_norm_g = gain(ks[21], (N_ODD, KV_LORA))
    mla_w_ukv = nrm(ks[22], (N_ODD, KV_LORA, MLA_HEADS * (QK_NOPE + V_DIM)), KV_LORA ** -0.5)
    mla_w_o = nrm(ks[23], (N_ODD, MLA_HEADS * V_DIM, D_MODEL), (MLA_HEADS * V_DIM) ** -0.5)
    final_norm_g = gain(ks[24], (D_MODEL,))
    return {"x": x, "c": c, "positions": positions, "mod_w": mod_w, "mod_b": mod_b,
            "norm_g": norm_g, "ffn_w13": ffn_w13, "ffn_w2": ffn_w2, "ab_w_in": ab_w_in,
            "pool_w": pool_w, "pool_scale": pool_scale, "ssd_conv_w": ssd_conv_w,
            "ssd_conv_b": ssd_conv_b, "ssd_dt_bias": ssd_dt_bias, "ssd_a_log": ssd_a_log,
            "ssd_d": ssd_d, "ssd_norm_g": ssd_norm_g, "ab_w_out": ab_w_out,
            "mla_w_in": mla_w_in, "mla_q_norm_g": mla_q_norm_g, "mla_w_uq": mla_w_uq,
            "mla_kv_norm_g": mla_kv_norm_g, "mla_w_ukv": mla_w_ukv, "mla_w_o": mla_w_o,
            "final_norm_g": final_norm_g}


def reference(x, c, positions, mod_w, mod_b, norm_g, ffn_w13, ffn_w2, ab_w_in, pool_w,
              pool_scale, ssd_conv_w, ssd_conv_b, ssd_dt_bias, ssd_a_log, ssd_d, ssd_norm_g,
              ab_w_out, mla_w_in, mla_q_norm_g, mla_w_uq, mla_kv_norm_g, mla_w_ukv, mla_w_o,
              final_norm_g):
    c_act = jax.nn.silu(c)
    for i in range(DEPTH):
        mod = c_act @ mod_w[i] + mod_b[i]
        sh1, sc1, g1, sh2, sc2, g2, sh3, sc3, g3 = jnp.split(mod, 3 * N_SUB, axis=-1)
        h = modulate(rmsnorm(x, norm_g[i, 0]), sh1, sc1)
        x = x + FFN_RES * g1[:, None, :] * swiglu(h, ffn_w13[i, 0], ffn_w2[i, 0])
        h = modulate(rmsnorm(x, norm_g[i, 1]), sh2, sc2)
        j = i // 2
        if i % 2 == 0:
            m = pool_ssd_mixer(h, ab_w_in[j], pool_w[j], pool_scale[j], ssd_conv_w[j],
                               ssd_conv_b[j], ssd_dt_bias[j], ssd_a_log[j], ssd_d[j],
                               ssd_norm_g[j], ab_w_out[j])
        else:
            m = mla_mixer(h, positions, mla_w_in[j], mla_q_norm_g[j], mla_w_uq[j],
                          mla_kv_norm_g[j], mla_w_ukv[j], mla_w_o[j])
        x = x + g2[:, None, :] * m
        h = modulate(rmsnorm(x, norm_g[i, 2]), sh3, sc3)
        x = x + FFN_RES * g3[:, None, :] * swiglu(h, ffn_w13[i, 1], ffn_w2[i, 1])
    return rmsnorm(x, final_norm_g)
```

```python
import functools
import math

import jax
import jax.numpy as jnp
from jax import lax
from jax.experimental import pallas as pl
from jax.experimental.pallas import tpu as pltpu

F32 = jnp.float32
BF16 = jnp.bfloat16

RMS_EPS = 1e-6
FFN_RES = 0.5
POOL_WINDOWS = (2, 4, 8, 16)
SSD_HEADDIM = 64
SSD_GROUPS = 2
D_STATE = 128
CONV_K = 4
SSD_CHUNK = 128
MLA_HEADS = 16
QK_NOPE = 64
QK_ROPE = 32
V_DIM = 64
ROPE_THETA = 10000.0
N_MOD = 9

LANES = 128
SUBLANES = 8
VMEM_LIMIT_BYTES = 56 * 1024 * 1024

TM_FFN = 512
FFN_CHUNK = 256
TM_PROJ = 512
TM_MIX = 256
TM_MOD = 1024
TQ = 256
TK = 256
HALF_ROPE = QK_ROPE // 2
HEAD_PAD = 128
ROPE_LANE0 = QK_NOPE


def _params(*sem):
    return pltpu.CompilerParams(dimension_semantics=sem, vmem_limit_bytes=VMEM_LIMIT_BYTES)


def _const_spec(shape):
    nd = len(shape)
    return pl.BlockSpec(shape, lambda *_: (0,) * nd, pipeline_mode=pl.Buffered(1))


def _silu(a):
    return a * jax.nn.sigmoid(a)


def _dot(a, b):
    return jnp.dot(a, b, preferred_element_type=F32)


def _dot_nt(a, b):
    return lax.dot_general(a, b, (((1,), (1,)), ((), ())), preferred_element_type=F32)


def _norm_mod(x, g, shift, scale):
    ms = jnp.mean(x * x, axis=-1, keepdims=True)
    y = x * lax.rsqrt(ms + RMS_EPS) * g
    return y * (1.0 + scale) + shift


def _split_bf16(x, parts):
    out = []
    r = x
    for _ in range(parts):
        p = r.astype(BF16)
        out.append(p)
        r = r - p.astype(F32)
    return out


def _mod_kernel(c_ref, w_ref, b_ref, o_ref):
    ca = _silu(c_ref[...])
    o_ref[...] = jnp.dot(ca, w_ref[...], preferred_element_type=F32,
                         precision=lax.Precision.HIGHEST) + b_ref[...]


def _modulation(c, mod_w, mod_b):
    depth, d, n = mod_w.shape
    bsz = c.shape[0]
    out = pl.pallas_call(
        _mod_kernel,
        grid=(depth, n // TM_MOD),
        in_specs=[
            pl.BlockSpec((bsz, d), lambda i, j: (0, 0)),
            pl.BlockSpec((None, d, TM_MOD), lambda i, j: (i, 0, j)),
            pl.BlockSpec((None, 1, TM_MOD), lambda i, j: (i, 0, j)),
        ],
        out_specs=pl.BlockSpec((None, bsz, TM_MOD), lambda i, j: (i, 0, j)),
        out_shape=jax.ShapeDtypeStruct((depth, bsz, n), F32),
        compiler_params=_params("arbitrary", "arbitrary"),
        name="adaln_modulation",
    )(c, mod_w, mod_b.reshape(depth, 1, n))
    return out.reshape(depth, bsz, N_MOD, d)


def _ffn_kernel(*refs, sub, n_chunks, final):
    if final:
        x_ref, mod_ref, g_ref, w1_ref, w3_ref, w2_ref, fg_ref, o_ref, h_ref, acc_ref = refs
    else:
        x_ref, mod_ref, g_ref, w1_ref, w3_ref, w2_ref, o_ref, h_ref, acc_ref = refs
    x = x_ref[...]
    shift = mod_ref[3 * sub:3 * sub + 1, :]
    scale = mod_ref[3 * sub + 1:3 * sub + 2, :]
    gate = mod_ref[3 * sub + 2:3 * sub + 3, :]
    h_ref[...] = _norm_mod(x, g_ref[...], shift, scale).astype(BF16)
    acc_ref[...] = jnp.zeros_like(acc_ref)

    def chunk(ci, carry):
        h = h_ref[...]
        a = _dot(h, w1_ref[ci])
        b = _dot(h, w3_ref[ci])
        act = (_silu(a) * b).astype(BF16)
        acc_ref[...] += _dot(act, w2_ref[ci])
        return carry

    lax.fori_loop(0, n_chunks, chunk, 0)
    y = x + (FFN_RES * gate) * acc_ref[...]
    if final:
        ms = jnp.mean(y * y, axis=-1, keepdims=True)
        y = y * lax.rsqrt(ms + RMS_EPS) * fg_ref[...]
    o_ref[...] = y


def _ffn(x, modl, g, w13, w2, sub, final_g=None):
    bsz, seq, d = x.shape
    f = w2.shape[0]
    n_chunks = f // FFN_CHUNK
    w1 = w13[:, :f].reshape(d, n_chunks, FFN_CHUNK).transpose(1, 0, 2).astype(BF16)
    w3 = w13[:, f:].reshape(d, n_chunks, FFN_CHUNK).transpose(1, 0, 2).astype(BF16)
    w2c = w2.reshape(n_chunks, FFN_CHUNK, d).astype(BF16)
    final = final_g is not None
    in_specs = [
        pl.BlockSpec((None, TM_FFN, d), lambda b, l: (b, l, 0)),
        pl.BlockSpec((None, N_MOD, d), lambda b, l: (b, 0, 0)),
        _const_spec((1, d)),
        _const_spec((n_chunks, d, FFN_CHUNK)),
        _const_spec((n_chunks, d, FFN_CHUNK)),
        _const_spec((n_chunks, FFN_CHUNK, d)),
    ]
    args = [x, modl, g.reshape(1, d), w1, w3, w2c]
    if final:
        in_specs.append(_const_spec((1, d)))
        args.append(final_g.reshape(1, d))
    return pl.pallas_call(
        functools.partial(_ffn_kernel, sub=sub, n_chunks=n_chunks, final=final),
        grid=(bsz, seq // TM_FFN),
        in_specs=in_specs,
        out_specs=pl.BlockSpec((None, TM_FFN, d), lambda b, l: (b, l, 0)),
        out_shape=jax.ShapeDtypeStruct((bsz, seq, d), F32),
        scratch_shapes=[pltpu.VMEM((TM_FFN, d), BF16), pltpu.VMEM((TM_FFN, d), F32)],
        compiler_params=_params("arbitrary", "arbitrary"),
        name="swiglu_ffn_final" if final else "swiglu_ffn",
    )(*args)


def _out_proj_kernel(x_ref, y_ref, mod_ref, w_ref, o_ref):
    gate = mod_ref[5:6, :]
    o_ref[...] = x_ref[...] + gate * _dot(y_ref[...], w_ref[...])


def _out_proj(x, y, modl, w):
    bsz, seq, d = x.shape
    k = y.shape[-1]
    return pl.pallas_call(
        _out_proj_kernel,
        grid=(bsz, seq // TM_PROJ),
        in_specs=[
            pl.BlockSpec((None, TM_PROJ, d), lambda b, l: (b, l, 0)),
            pl.BlockSpec((None, TM_PROJ, k), lambda b, l: (b, l, 0)),
            pl.BlockSpec((None, N_MOD, d), lambda b, l: (b, 0, 0)),
            _const_spec((k, d)),
        ],
        out_specs=pl.BlockSpec((None, TM_PROJ, d), lambda b, l: (b, l, 0)),
        out_shape=jax.ShapeDtypeStruct((bsz, seq, d), F32),
        compiler_params=_params("arbitrary", "arbitrary"),
        name="mixer_out_proj",
    )(x, y, modl, w.astype(BF16))


def _ab_in_kernel(x_ref, mod_ref, g_ref, w_ref, u_ref, z_ref, xbc_ref, dt_ref, *, widths):
    h = _norm_mod(x_ref[...], g_ref[...], mod_ref[3:4, :], mod_ref[4:5, :]).astype(BF16)
    p = _dot(h, w_ref[...])
    c0 = 0
    for ref, wd in zip((u_ref, z_ref, xbc_ref, dt_ref), widths):
        ref[...] = p[:, c0:c0 + wd]
        c0 += wd


def _ab_in_proj(x, modl, g, w_in, pool_dim, d_ssm, conv_dim, n_heads):
    bsz, seq, d = x.shape
    main = pool_dim + d_ssm + conv_dim
    w_dt = jnp.pad(w_in[:, main:main + n_heads], ((0, 0), (0, LANES - n_heads)))
    w = jnp.concatenate([w_in[:, :main], w_dt], axis=1).astype(BF16)
    widths = (pool_dim, d_ssm, conv_dim, LANES)
    return pl.pallas_call(
        functools.partial(_ab_in_kernel, widths=widths),
        grid=(bsz, seq // TM_PROJ),
        in_specs=[
            pl.BlockSpec((None, TM_PROJ, d), lambda b, l: (b, l, 0)),
            pl.BlockSpec((None, N_MOD, d), lambda b, l: (b, 0, 0)),
            _const_spec((1, d)),
            _const_spec(w.shape),
        ],
        out_specs=[pl.BlockSpec((None, TM_PROJ, wd), lambda b, l: (b, l, 0)) for wd in widths],
        out_shape=[jax.ShapeDtypeStruct((bsz, seq, wd), F32) for wd in widths],
        compiler_params=_params("arbitrary", "arbitrary"),
        name="pool_ssd_in_proj",
    )(x, modl, g.reshape(1, d), w)


def _softplus(x):
    return jnp.maximum(x, 0.0) + jnp.log1p(jnp.exp(-jnp.abs(x)))


def _mixer_kernel(u_ref, z_ref, xbc_ref, dtr_ref, pw_ref, ps_ref, cw_ref, cb_ref, dtb_ref,
                  aneg_ref, dsk_ref, ng_ref, rexp_ref, o_ref, xext, uext, st_ref, xc_ref,
                  *, pool_dim, d_ssm):
    tm = u_ref.shape[0]
    q = SSD_CHUNK
    n_groups = SSD_GROUPS
    gdim = d_ssm // n_groups
    heads_per_group = gdim // SSD_HEADDIM
    pool_gdim = pool_dim // len(POOL_WINDOWS)
    max_win = max(POOL_WINDOWS)
    lt = pl.program_id(1)

    @pl.when(lt == 0)
    def _():
        xext[0:SUBLANES, :] = jnp.zeros((SUBLANES, xext.shape[1]), F32)
        uext[0:max_win, :] = jnp.zeros((max_win, uext.shape[1]), F32)
        st_ref[...] = jnp.zeros_like(st_ref)

    xext[SUBLANES:SUBLANES + tm, :] = xbc_ref[...]
    acc = cb_ref[...] + cw_ref[CONV_K - 1:CONV_K, :] * xbc_ref[...]
    for k in range(CONV_K - 1):
        off = SUBLANES - (CONV_K - 1) + k
        acc = acc + cw_ref[k:k + 1, :] * xext[off:off + tm, :]
    xc_ref[...] = _silu(acc)
    xext[0:SUBLANES, :] = xext[tm:tm + SUBLANES, :]

    uext[max_win:max_win + tm, :] = u_ref[...]
    tpos = (lt * tm + lax.broadcasted_iota(jnp.int32, (tm, pool_gdim), 0)).astype(F32)
    for gi, win in enumerate(POOL_WINDOWS):
        cols = slice(gi * pool_gdim, (gi + 1) * pool_gdim)
        cur = u_ref[:, cols]
        ws = cur
        for k in range(1, win):
            ws = ws + uext[max_win - k:max_win - k + tm, cols]
        count = jnp.minimum(tpos + 1.0, float(win))
        dlt = ws / count - cur
        yp = _dot(dlt.astype(BF16), pw_ref[gi]) * ps_ref[:, cols]
        o_ref[:, cols] = yp.astype(BF16)
    uext[0:max_win, :] = uext[tm:tm + max_win, :]

    row = lax.broadcasted_iota(jnp.int32, (q, q), 0)
    col = lax.broadcasted_iota(jnp.int32, (q, q), 1)
    causal = row >= col
    tril = jnp.where(causal, 1.0, 0.0).astype(BF16)
    lane = lax.broadcasted_iota(jnp.int32, (1, LANES), 1)
    lo_half = lane < SSD_HEADDIM
    rexp = rexp_ref[...]

    def expand(v):
        hi, lo = _split_bf16(v, 2)
        return _dot(hi, rexp) + _dot(lo, rexp)

    for ci in range(tm // q):
        r0 = ci * q
        xs = xc_ref[r0:r0 + q, 0:d_ssm]
        dt = _softplus(dtr_ref[r0:r0 + q, :] + dtb_ref[...])
        adt = dt * aneg_ref[...]
        acs = sum(_dot(tril, p) for p in _split_bf16(adt, 3))
        acs_t = acs.T
        a_last = acs[q - 1:q, :]
        dt_e = expand(dt)
        eacs_e = expand(jnp.exp(acs))
        wend_e = expand(jnp.exp(a_last - acs) * dt)
        xdt = xs * dt_e
        xw = xs * wend_e
        for g in range(n_groups):
            gc = slice(g * gdim, (g + 1) * gdim)
            b0 = d_ssm + g * D_STATE
            c0 = d_ssm + n_groups * D_STATE + g * D_STATE
            bm = xc_ref[r0:r0 + q, b0:b0 + D_STATE]
            cm = xc_ref[r0:r0 + q, c0:c0 + D_STATE].astype(BF16)
            cb = _dot_nt(cm, bm.astype(BF16))
            s_prev = st_ref[g]
            y_off = _dot(cm, s_prev.astype(BF16)) * eacs_e[:, gc]
            ys = []
            for pr in range(heads_per_group // 2):
                h0 = g * heads_per_group + 2 * pr
                gs = []
                for hh in range(2):
                    h = h0 + hh
                    diff = acs[:, h:h + 1] - acs_t[h:h + 1, :]
                    dec = jnp.exp(jnp.where(causal, diff, -jnp.inf))
                    gs.append((cb * dec).astype(BF16))
                xp = xdt[:, h0 * SSD_HEADDIM:(h0 + 2) * SSD_HEADDIM]
                xbd = jnp.concatenate([jnp.where(lo_half, xp, 0.0), jnp.where(lo_half, 0.0, xp)],
                                      axis=0).astype(BF16)
                ys.append(_dot(jnp.concatenate(gs, axis=1), xbd))
            y = jnp.concatenate(ys, axis=1) + y_off
            st_ref[g] = eacs_e[q - 1:q, gc] * s_prev + _dot(bm.T.astype(BF16), xw[:, gc].astype(BF16))
            y = y + dsk_ref[:, gc] * xs[:, gc]
            y = y * _silu(z_ref[r0:r0 + q, gc])
            ms = jnp.mean(y * y, axis=-1, keepdims=True)
            y = y * lax.rsqrt(ms + RMS_EPS) * ng_ref[:, gc]
            o_ref[r0:r0 + q, pool_dim + g * gdim:pool_dim + (g + 1) * gdim] = y.astype(BF16)


def _mixer_core(u, z, xbc, dtr, pool_w, pool_scale, conv_w, conv_b, dt_bias, a_log, d_skip, norm_g):
    bsz, seq, pool_dim = u.shape
    d_ssm = z.shape[-1]
    conv_dim = xbc.shape[-1]
    n_heads = dt_bias.shape[0]
    pad = LANES - n_heads
    dtb = jnp.pad(dt_bias.astype(F32), (0, pad)).reshape(1, LANES)
    aneg = jnp.pad(-jnp.exp(a_log.astype(F32)), (0, pad)).reshape(1, LANES)
    dsk = jnp.repeat(d_skip.astype(F32), SSD_HEADDIM).reshape(1, d_ssm)
    rexp = (jnp.arange(LANES)[:, None] == (jnp.arange(d_ssm) // SSD_HEADDIM)[None, :]).astype(BF16)
    consts = [pool_w.astype(BF16), pool_scale.reshape(1, pool_dim), conv_w, conv_b.reshape(1, conv_dim),
              dtb, aneg, dsk, norm_g.reshape(1, d_ssm), rexp]
    widths = (pool_dim, d_ssm, conv_dim, LANES)
    max_win = max(POOL_WINDOWS)
    return pl.pallas_call(
        functools.partial(_mixer_kernel, pool_dim=pool_dim, d_ssm=d_ssm),
        grid=(bsz, seq // TM_MIX),
        in_specs=[pl.BlockSpec((None, TM_MIX, wd), lambda b, l: (b, l, 0)) for wd in widths]
        + [_const_spec(a.shape) for a in consts],
        out_specs=pl.BlockSpec((None, TM_MIX, pool_dim + d_ssm), lambda b, l: (b, l, 0)),
        out_shape=jax.ShapeDtypeStruct((bsz, seq, pool_dim + d_ssm), BF16),
        scratch_shapes=[
            pltpu.VMEM((SUBLANES + TM_MIX, conv_dim), F32),
            pltpu.VMEM((max_win + TM_MIX, pool_dim), F32),
            pltpu.VMEM((SSD_GROUPS, D_STATE, d_ssm // SSD_GROUPS), F32),
            pltpu.VMEM((TM_MIX, conv_dim), F32),
        ],
        compiler_params=_params("arbitrary", "arbitrary"),
        name="pool_ssd_core",
    )(u, z, xbc, dtr, *consts)


def _mla_proj_kernel(x_ref, mod_ref, g_ref, pos_ref, win_ref, qg_ref, wuq_ref, kvg_ref, wk_ref,
                     wv_ref, invf_ref, q_ref, k_ref, v_ref, *, q_lora, kv_lora, scale):
    h = _norm_mod(x_ref[...], g_ref[...], mod_ref[3:4, :], mod_ref[4:5, :]).astype(BF16)
    p = _dot(h, win_ref[...])
    q_a = p[:, 0:q_lora]
    kv_a = p[:, q_lora:q_lora + kv_lora]
    k_r = p[:, q_lora + kv_lora:q_lora + kv_lora + HEAD_PAD]

    def rms(t, gain):
        return t * lax.rsqrt(jnp.mean(t * t, axis=-1, keepdims=True) + RMS_EPS) * gain

    qn = rms(q_a, qg_ref[...]).astype(BF16)
    kvn = rms(kv_a, kvg_ref[...]).astype(BF16)

    ang = pos_ref[...] * invf_ref[...]
    cs = jnp.cos(ang)
    sn = jnp.sin(ang)
    lane = lax.broadcasted_iota(jnp.int32, (1, HEAD_PAD), 1)
    first = (lane >= ROPE_LANE0) & (lane < ROPE_LANE0 + HALF_ROPE)
    second = (lane >= ROPE_LANE0 + HALF_ROPE) & (lane < ROPE_LANE0 + QK_ROPE)
    s_first = jnp.where(first, -sn, 0.0)
    s_second = jnp.where(second, sn, 0.0)

    def rope(t, mult):
        up = pltpu.roll(t, HEAD_PAD - HALF_ROPE, 1)
        down = pltpu.roll(t, HALF_ROPE, 1)
        return (t * cs + up * s_first + down * s_second) * mult

    qf = _dot(qn, wuq_ref[...])
    kf = _dot(kvn, wk_ref[...])
    k_rot = rope(k_r, 1.0)
    n_heads = q_ref.shape[0]
    for hd in range(n_heads):
        cols = slice(hd * HEAD_PAD, (hd + 1) * HEAD_PAD)
        q_ref[hd] = rope(qf[:, cols], scale).astype(BF16)
        k_ref[hd] = (kf[:, cols] + k_rot).astype(BF16)
    vf = _dot(kvn, wv_ref[...])
    for hp in range(v_ref.shape[0]):
        v_ref[hp] = vf[:, hp * LANES:(hp + 1) * LANES].astype(BF16)


def _mla_proj(x, modl, g, positions, w_in, q_norm_g, w_uq, kv_norm_g, w_ukv):
    bsz, seq, d = x.shape
    q_lora = w_uq.shape[0]
    kv_lora = w_ukv.shape[0]
    nh = MLA_HEADS
    w_rope = jnp.pad(w_in[:, q_lora + kv_lora:], ((0, 0), (ROPE_LANE0, HEAD_PAD - ROPE_LANE0 - QK_ROPE)))
    win = jnp.concatenate([w_in[:, :q_lora + kv_lora], w_rope], axis=1).astype(BF16)
    wuq = jnp.pad(w_uq.reshape(q_lora, nh, QK_NOPE + QK_ROPE),
                  ((0, 0), (0, 0), (0, HEAD_PAD - QK_NOPE - QK_ROPE))).reshape(q_lora, nh * HEAD_PAD)
    wkv = w_ukv.reshape(kv_lora, nh, QK_NOPE + V_DIM)
    wk = jnp.pad(wkv[:, :, :QK_NOPE], ((0, 0), (0, 0), (0, HEAD_PAD - QK_NOPE))).reshape(kv_lora, nh * HEAD_PAD)
    wv = wkv[:, :, QK_NOPE:].reshape(kv_lora, nh * V_DIM)
    inv_freq = ROPE_THETA ** (-jnp.arange(0, QK_ROPE, 2, dtype=F32) / QK_ROPE)
    invf = jnp.zeros((HEAD_PAD,), F32).at[ROPE_LANE0:ROPE_LANE0 + QK_ROPE].set(jnp.tile(inv_freq, 2))
    pos = positions.astype(F32).reshape(bsz, seq, 1)
    scale = 1.0 / math.sqrt(QK_NOPE + QK_ROPE)
    consts = [win, q_norm_g.reshape(1, q_lora), wuq.astype(BF16), kv_norm_g.reshape(1, kv_lora),
              wk.astype(BF16), wv.astype(BF16), invf.reshape(1, HEAD_PAD)]
    n_vp = nh * V_DIM // LANES
    return pl.pallas_call(
        functools.partial(_mla_proj_kernel, q_lora=q_lora, kv_lora=kv_lora, scale=scale),
        grid=(bsz, seq // TM_PROJ),
        in_specs=[
            pl.BlockSpec((None, TM_PROJ, d), lambda b, l: (b, l, 0)),
            pl.BlockSpec((None, N_MOD, d), lambda b, l: (b, 0, 0)),
            _const_spec((1, d)),
            pl.BlockSpec((None, TM_PROJ, 1), lambda b, l: (b, l, 0)),
        ] + [_const_spec(a.shape) for a in consts],
        out_specs=[
            pl.BlockSpec((None, nh, TM_PROJ, HEAD_PAD), lambda b, l: (b, 0, l, 0)),
            pl.BlockSpec((None, nh, TM_PROJ, HEAD_PAD), lambda b, l: (b, 0, l, 0)),
            pl.BlockSpec((None, n_vp, TM_PROJ, LANES), lambda b, l: (b, 0, l, 0)),
        ],
        out_shape=[
            jax.ShapeDtypeStruct((bsz, nh, seq, HEAD_PAD), BF16),
            jax.ShapeDtypeStruct((bsz, nh, seq, HEAD_PAD), BF16),
            jax.ShapeDtypeStruct((bsz, n_vp, seq, LANES), BF16),
        ],
        compiler_params=_params("arbitrary", "arbitrary"),
        name="mla_qkv_proj",
    )(x, modl, g.reshape(1, d), pos, *consts)


def _attn_kernel(q_ref, k_ref, v_ref, o_ref):
    seq = v_ref.shape[0]
    lane = lax.broadcasted_iota(jnp.int32, (1, LANES), 1)
    lo_half = lane < V_DIM
    row = lax.broadcasted_iota(jnp.int32, (TQ, TK), 0)
    col = lax.broadcasted_iota(jnp.int32, (TQ, TK), 1)
    steps = TQ // TK

    def q_tile(qi, carry):
        q0 = pl.multiple_of(qi * TQ, TQ)
        qs = [q_ref[hh, pl.ds(q0, TQ), :] for hh in range(2)]

        def kv_step(j, state, masked):
            k0 = pl.multiple_of(j * TK, TK)
            v = v_ref[pl.ds(k0, TK), :]
            zero = jnp.zeros_like(v)
            v_heads = (jnp.where(lo_half, v, zero), jnp.where(lo_half, zero, v))
            acc = state[4]
            new = []
            alphas = []
            pv = None
            for hh in range(2):
                m_old, l_old = state[2 * hh], state[2 * hh + 1]
                s = _dot_nt(qs[hh], k_ref[hh, pl.ds(k0, TK), :])
                if masked:
                    s = jnp.where(row + q0 >= col + k0, s, -jnp.inf)
                m_new = jnp.maximum(m_old, jnp.max(s, axis=-1, keepdims=True))
                alpha = jnp.exp(m_old - m_new)
                p = jnp.exp(s - m_new)
                new += [m_new, alpha * l_old + jnp.sum(p, axis=-1, keepdims=True)]
                alphas.append(alpha)
                d = _dot(p.astype(BF16), v_heads[hh])
                pv = d if pv is None else pv + d
            acc = acc * jnp.where(lo_half, alphas[0], alphas[1]) + pv
            return (*new, acc)

        init = (jnp.full((TQ, 1), -jnp.inf, F32), jnp.zeros((TQ, 1), F32),
                jnp.full((TQ, 1), -jnp.inf, F32), jnp.zeros((TQ, 1), F32),
                jnp.zeros((TQ, LANES), F32))
        state = lax.fori_loop(0, qi * steps, functools.partial(kv_step, masked=False), init)
        for jj in range(steps):
            state = kv_step(qi * steps + jj, state, masked=True)
        o_ref[pl.ds(q0, TQ), :] = (state[4] / jnp.where(lo_half, state[1], state[3])).astype(BF16)
        return carry

    lax.fori_loop(0, seq // TQ, q_tile, 0)


def _attention(q, k, v):
    bsz, nh, seq, _ = q.shape
    n_pairs = nh // 2
    return pl.pallas_call(
        _attn_kernel,
        grid=(bsz, n_pairs),
        in_specs=[
            pl.BlockSpec((None, 2, seq, HEAD_PAD), lambda b, hp: (b, hp, 0, 0)),
            pl.BlockSpec((None, 2, seq, HEAD_PAD), lambda b, hp: (b, hp, 0, 0)),
            pl.BlockSpec((None, None, seq, LANES), lambda b, hp: (b, hp, 0, 0)),
        ],
        out_specs=pl.BlockSpec((None, seq, LANES), lambda b, hp: (b, 0, hp)),
        out_shape=jax.ShapeDtypeStruct((bsz, seq, nh * V_DIM), BF16),
        compiler_params=_params("arbitrary", "arbitrary"),
        name="mla_causal_attention",
    )(q, k, v)


def kernel(x, c, positions, mod_w, mod_b, norm_g, ffn_w13, ffn_w2, ab_w_in, pool_w, pool_scale, ssd_conv_w, ssd_conv_b, ssd_dt_bias, ssd_a_log, ssd_d, ssd_norm_g, ab_w_out, mla_w_in, mla_q_norm_g, mla_w_uq, mla_kv_norm_g, mla_w_ukv, mla_w_o, final_norm_g):
    depth = mod_w.shape[0]
    mod = _modulation(c, mod_w, mod_b)
    for i in range(depth):
        modl = mod[i]
        j = i // 2
        x = _ffn(x, modl, norm_g[i, 0], ffn_w13[i, 0], ffn_w2[i, 0], sub=0)
        if i % 2 == 0:
            pool_dim = pool_scale.shape[-1]
            d_ssm = ssd_norm_g.shape[-1]
            conv_dim = ssd_conv_b.shape[-1]
            n_heads = ssd_dt_bias.shape[-1]
            u, z, xbc, dtr = _ab_in_proj(x, modl, norm_g[i, 1], ab_w_in[j], pool_dim, d_ssm, conv_dim, n_heads)
            y = _mixer_core(u, z, xbc, dtr, pool_w[j], pool_scale[j], ssd_conv_w[j], ssd_conv_b[j],
                            ssd_dt_bias[j], ssd_a_log[j], ssd_d[j], ssd_norm_g[j])
            x = _out_proj(x, y, modl, ab_w_out[j])
        else:
            q, k, v = _mla_proj(x, modl, norm_g[i, 1], positions, mla_w_in[j], mla_q_norm_g[j],
                                mla_w_uq[j], mla_kv_norm_g[j], mla_w_ukv[j])
            o = _attention(q, k, v)
            x = _out_proj(x, o, modl, mla_w_o[j])
        last = i == depth - 1
        x = _ffn(x, modl, norm_g[i, 2], ffn_w13[i, 1], ffn_w2[i, 1], sub=2,
                 final_g=final_norm_g if last else None)
    return x
```

```python
import functools
import math

import jax
import jax.numpy as jnp
from jax import lax
from jax.experimental import pallas as pl
from jax.experimental.pallas import tpu as pltpu

F32 = jnp.float32
BF16 = jnp.bfloat16

RMS_EPS = 1e-6
FFN_RES = 0.5
POOL_WINDOWS = (2, 4, 8, 16)
SSD_HEADDIM = 64
SSD_GROUPS = 2
D_STATE = 128
CONV_K = 4
SSD_CHUNK = 128
MLA_HEADS = 16
QK_NOPE = 64
QK_ROPE = 32
V_DIM = 64
ROPE_THETA = 10000.0
N_MOD = 9

LANES = 128
SUBLANES = 8
VMEM_LIMIT_BYTES = 56 * 1024 * 1024

TM_FFN = 512
FFN_CHUNK = 256
TM_PROJ = 512
TM_MIX = 256
TM_MOD = 1024
TQ = 512
TK = 512
TD = 256
HALF_ROPE = QK_ROPE // 2
HEAD_PAD = 128
ROPE_LANE0 = QK_NOPE


def _params(*sem):
    return pltpu.CompilerParams(dimension_semantics=sem, vmem_limit_bytes=VMEM_LIMIT_BYTES)


def _const_spec(shape):
    nd = len(shape)
    return pl.BlockSpec(shape, lambda *_: (0,) * nd, pipeline_mode=pl.Buffered(1))


def _silu(a):
    return a * jax.nn.sigmoid(a)


def _dot(a, b):
    return jnp.dot(a, b, preferred_element_type=F32)


def _dot_nt(a, b):
    return lax.dot_general(a, b, (((1,), (1,)), ((), ())), preferred_element_type=F32)


def _norm_mod(x, g, shift, scale):
    ms = jnp.mean(x * x, axis=-1, keepdims=True)
    y = x * lax.rsqrt(ms + RMS_EPS) * g
    return y * (1.0 + scale) + shift


def _split_bf16(x, parts):
    out = []
    r = x
    for _ in range(parts):
        p = r.astype(BF16)
        out.append(p)
        r = r - p.astype(F32)
    return out


def _mod_kernel(c_ref, w_ref, b_ref, o_ref):
    ca = _silu(c_ref[...])
    o_ref[...] = jnp.dot(ca, w_ref[...], preferred_element_type=F32,
                         precision=lax.Precision.HIGHEST) + b_ref[...]


def _modulation(c, mod_w, mod_b):
    depth, d, n = mod_w.shape
    bsz = c.shape[0]
    out = pl.pallas_call(
        _mod_kernel,
        grid=(depth, n // TM_MOD),
        in_specs=[
            pl.BlockSpec((bsz, d), lambda i, j: (0, 0)),
            pl.BlockSpec((None, d, TM_MOD), lambda i, j: (i, 0, j)),
            pl.BlockSpec((None, 1, TM_MOD), lambda i, j: (i, 0, j)),
        ],
        out_specs=pl.BlockSpec((None, bsz, TM_MOD), lambda i, j: (i, 0, j)),
        out_shape=jax.ShapeDtypeStruct((depth, bsz, n), F32),
        compiler_params=_params("arbitrary", "arbitrary"),
        name="adaln_modulation",
    )(c, mod_w, mod_b.reshape(depth, 1, n))
    return out.reshape(depth, bsz, N_MOD, d)


def _ffn_kernel(*refs, sub, n_chunks, final):
    if final:
        x_ref, mod_ref, g_ref, w1_ref, w3_ref, w2_ref, fg_ref, o_ref, h_ref, acc_ref = refs
    else:
        x_ref, mod_ref, g_ref, w1_ref, w3_ref, w2_ref, o_ref, h_ref, acc_ref = refs
    x = x_ref[...]
    shift = mod_ref[3 * sub:3 * sub + 1, :]
    scale = mod_ref[3 * sub + 1:3 * sub + 2, :]
    gate = mod_ref[3 * sub + 2:3 * sub + 3, :]
    h_ref[...] = _norm_mod(x, g_ref[...], shift, scale).astype(BF16)
    acc_ref[...] = jnp.zeros_like(acc_ref)

    def chunk(ci, carry):
        h = h_ref[...]
        a = _dot(h, w1_ref[ci])
        b = _dot(h, w3_ref[ci])
        act = (_silu(a) * b).astype(BF16)
        acc_ref[...] += _dot(act, w2_ref[ci])
        return carry

    lax.fori_loop(0, n_chunks, chunk, 0)
    y = x + (FFN_RES * gate) * acc_ref[...]
    if final:
        ms = jnp.mean(y * y, axis=-1, keepdims=True)
        y = y * lax.rsqrt(ms + RMS_EPS) * fg_ref[...]
    o_ref[...] = y


def _ffn(x, modl, g, w13, w2, sub, final_g=None):
    bsz, seq, d = x.shape
    f = w2.shape[0]
    n_chunks = f // FFN_CHUNK
    w1 = w13[:, :f].reshape(d, n_chunks, FFN_CHUNK).transpose(1, 0, 2).astype(BF16)
    w3 = w13[:, f:].reshape(d, n_chunks, FFN_CHUNK).transpose(1, 0, 2).astype(BF16)
    w2c = w2.reshape(n_chunks, FFN_CHUNK, d).astype(BF16)
    final = final_g is not None
    in_specs = [
        pl.BlockSpec((None, TM_FFN, d), lambda b, l: (b, l, 0)),
        pl.BlockSpec((None, N_MOD, d), lambda b, l: (b, 0, 0)),
        _const_spec((1, d)),
        _const_spec((n_chunks, d, FFN_CHUNK)),
        _const_spec((n_chunks, d, FFN_CHUNK)),
        _const_spec((n_chunks, FFN_CHUNK, d)),
    ]
    args = [x, modl, g.reshape(1, d), w1, w3, w2c]
    if final:
        in_specs.append(_const_spec((1, d)))
        args.append(final_g.reshape(1, d))
    return pl.pallas_call(
        functools.partial(_ffn_kernel, sub=sub, n_chunks=n_chunks, final=final),
        grid=(bsz, seq // TM_FFN),
        in_specs=in_specs,
        out_specs=pl.BlockSpec((None, TM_FFN, d), lambda b, l: (b, l, 0)),
        out_shape=jax.ShapeDtypeStruct((bsz, seq, d), F32),
        scratch_shapes=[pltpu.VMEM((TM_FFN, d), BF16), pltpu.VMEM((TM_FFN, d), F32)],
        compiler_params=_params("arbitrary", "arbitrary"),
        name="swiglu_ffn_final" if final else "swiglu_ffn",
    )(*args)


def _out_proj_kernel(x_ref, y_ref, mod_ref, w_ref, o_ref):
    gate = mod_ref[5:6, :]
    o_ref[...] = x_ref[...] + gate * _dot(y_ref[...], w_ref[...])


def _out_proj(x, y, modl, w):
    bsz, seq, d = x.shape
    k = y.shape[-1]
    return pl.pallas_call(
        _out_proj_kernel,
        grid=(bsz, seq // TM_PROJ),
        in_specs=[
            pl.BlockSpec((None, TM_PROJ, d), lambda b, l: (b, l, 0)),
            pl.BlockSpec((None, TM_PROJ, k), lambda b, l: (b, l, 0)),
            pl.BlockSpec((None, N_MOD, d), lambda b, l: (b, 0, 0)),
            _const_spec((k, d)),
        ],
        out_specs=pl.BlockSpec((None, TM_PROJ, d), lambda b, l: (b, l, 0)),
        out_shape=jax.ShapeDtypeStruct((bsz, seq, d), F32),
        compiler_params=_params("arbitrary", "arbitrary"),
        name="mixer_out_proj",
    )(x, y, modl, w.astype(BF16))


def _ab_in_kernel(x_ref, mod_ref, g_ref, w_ref, u_ref, z_ref, xbc_ref, dt_ref, *, widths):
    h = _norm_mod(x_ref[...], g_ref[...], mod_ref[3:4, :], mod_ref[4:5, :]).astype(BF16)
    p = _dot(h, w_ref[...])
    c0 = 0
    for ref, wd in zip((u_ref, z_ref, xbc_ref, dt_ref), widths):
        ref[...] = p[:, c0:c0 + wd]
        c0 += wd


def _ab_in_proj(x, modl, g, w_in, pool_dim, d_ssm, conv_dim, n_heads):
    bsz, seq, d = x.shape
    main = pool_dim + d_ssm + conv_dim
    w_dt = jnp.pad(w_in[:, main:main + n_heads], ((0, 0), (0, LANES - n_heads)))
    w = jnp.concatenate([w_in[:, :main], w_dt], axis=1).astype(BF16)
    widths = (pool_dim, d_ssm, conv_dim, LANES)
    return pl.pallas_call(
        functools.partial(_ab_in_kernel, widths=widths),
        grid=(bsz, seq // TM_PROJ),
        in_specs=[
            pl.BlockSpec((None, TM_PROJ, d), lambda b, l: (b, l, 0)),
            pl.BlockSpec((None, N_MOD, d), lambda b, l: (b, 0, 0)),
            _const_spec((1, d)),
            _const_spec(w.shape),
        ],
        out_specs=[pl.BlockSpec((None, TM_PROJ, wd), lambda b, l: (b, l, 0)) for wd in widths],
        out_shape=[jax.ShapeDtypeStruct((bsz, seq, wd), F32) for wd in widths],
        compiler_params=_params("arbitrary", "arbitrary"),
        name="pool_ssd_in_proj",
    )(x, modl, g.reshape(1, d), w)


def _softplus(x):
    return jnp.maximum(x, 0.0) + jnp.log1p(jnp.exp(-jnp.abs(x)))


def _mixer_kernel(u_ref, z_ref, xbc_ref, dtr_ref, pw_ref, ps_ref, cw_ref, cb_ref, dtb_ref,
                  aneg_ref, dsk_ref, ng_ref, rexp_ref, o_ref, xext, uext, st_ref, xc_ref,
                  *, pool_dim, d_ssm):
    tm = u_ref.shape[0]
    q = SSD_CHUNK
    n_groups = SSD_GROUPS
    gdim = d_ssm // n_groups
    heads_per_group = gdim // SSD_HEADDIM
    pool_gdim = pool_dim // len(POOL_WINDOWS)
    max_win = max(POOL_WINDOWS)
    lt = pl.program_id(1)

    @pl.when(lt == 0)
    def _():
        xext[0:SUBLANES, :] = jnp.zeros((SUBLANES, xext.shape[1]), F32)
        uext[0:max_win, :] = jnp.zeros((max_win, uext.shape[1]), F32)
        st_ref[...] = jnp.zeros_like(st_ref)

    xext[SUBLANES:SUBLANES + tm, :] = xbc_ref[...]
    acc = cb_ref[...] + cw_ref[CONV_K - 1:CONV_K, :] * xbc_ref[...]
    for k in range(CONV_K - 1):
        off = SUBLANES - (CONV_K - 1) + k
        acc = acc + cw_ref[k:k + 1, :] * xext[off:off + tm, :]
    xc_ref[...] = _silu(acc)
    xext[0:SUBLANES, :] = xext[tm:tm + SUBLANES, :]

    uext[max_win:max_win + tm, :] = u_ref[...]
    tpos = (lt * tm + lax.broadcasted_iota(jnp.int32, (tm, pool_gdim), 0)).astype(F32)
    for gi, win in enumerate(POOL_WINDOWS):
        cols = slice(gi * pool_gdim, (gi + 1) * pool_gdim)
        cur = u_ref[:, cols]
        ws = cur
        for k in range(1, win):
            ws = ws + uext[max_win - k:max_win - k + tm, cols]
        count = jnp.minimum(tpos + 1.0, float(win))
        dlt = ws / count - cur
        yp = _dot(dlt.astype(BF16), pw_ref[gi]) * ps_ref[:, cols]
        o_ref[:, cols] = yp.astype(BF16)
    uext[0:max_win, :] = uext[tm:tm + max_win, :]

    row = lax.broadcasted_iota(jnp.int32, (q, q), 0)
    col = lax.broadcasted_iota(jnp.int32, (q, q), 1)
    causal = row >= col
    tril = jnp.where(causal, 1.0, 0.0).astype(BF16)
    lane = lax.broadcasted_iota(jnp.int32, (1, LANES), 1)
    lo_half = lane < SSD_HEADDIM
    rexp = rexp_ref[...]

    def expand(v):
        hi, lo = _split_bf16(v, 2)
        return _dot(hi, rexp) + _dot(lo, rexp)

    for ci in range(tm // q):
        r0 = ci * q
        xs = xc_ref[r0:r0 + q, 0:d_ssm]
        dt = _softplus(dtr_ref[r0:r0 + q, :] + dtb_ref[...])
        adt = dt * aneg_ref[...]
        acs = sum(_dot(tril, p) for p in _split_bf16(adt, 3))
        acs_t = acs.T
        a_last = acs[q - 1:q, :]
        dt_e = expand(dt)
        eacs_e = expand(jnp.exp(acs))
        wend_e = expand(jnp.exp(a_last - acs) * dt)
        xdt = xs * dt_e
        xw = xs * wend_e
        for g in range(n_groups):
            gc = slice(g * gdim, (g + 1) * gdim)
            b0 = d_ssm + g * D_STATE
            c0 = d_ssm + n_groups * D_STATE + g * D_STATE
            bm = xc_ref[r0:r0 + q, b0:b0 + D_STATE]
            cm = xc_ref[r0:r0 + q, c0:c0 + D_STATE].astype(BF16)
            cb = _dot_nt(cm, bm.astype(BF16))
            s_prev = st_ref[g]
            y_off = _dot(cm, s_prev.astype(BF16)) * eacs_e[:, gc]
            ys = []
            for pr in range(heads_per_group // 2):
                h0 = g * heads_per_group + 2 * pr
                gs = []
                for hh in range(2):
                    h = h0 + hh
                    diff = acs[:, h:h + 1] - acs_t[h:h + 1, :]
                    dec = jnp.exp(jnp.where(causal, diff, -jnp.inf))
                    gs.append((cb * dec).astype(BF16))
                xp = xdt[:, h0 * SSD_HEADDIM:(h0 + 2) * SSD_HEADDIM]
                xbd = jnp.concatenate([jnp.where(lo_half, xp, 0.0), jnp.where(lo_half, 0.0, xp)],
                                      axis=0).astype(BF16)
                ys.append(_dot(jnp.concatenate(gs, axis=1), xbd))
            y = jnp.concatenate(ys, axis=1) + y_off
            st_ref[g] = eacs_e[q - 1:q, gc] * s_prev + _dot(bm.T.astype(BF16), xw[:, gc].astype(BF16))
            y = y + dsk_ref[:, gc] * xs[:, gc]
            y = y * _silu(z_ref[r0:r0 + q, gc])
            ms = jnp.mean(y * y, axis=-1, keepdims=True)
            y = y * lax.rsqrt(ms + RMS_EPS) * ng_ref[:, gc]
            o_ref[r0:r0 + q, pool_dim + g * gdim:pool_dim + (g + 1) * gdim] = y.astype(BF16)


def _mixer_core(u, z, xbc, dtr, pool_w, pool_scale, conv_w, conv_b, dt_bias, a_log, d_skip, norm_g):
    bsz, seq, pool_dim = u.shape
    d_ssm = z.shape[-1]
    conv_dim = xbc.shape[-1]
    n_heads = dt_bias.shape[0]
    pad = LANES - n_heads
    dtb = jnp.pad(dt_bias.astype(F32), (0, pad)).reshape(1, LANES)
    aneg = jnp.pad(-jnp.exp(a_log.astype(F32)), (0, pad)).reshape(1, LANES)
    dsk = jnp.repeat(d_skip.astype(F32), SSD_HEADDIM).reshape(1, d_ssm)
    rexp = (jnp.arange(LANES)[:, None] == (jnp.arange(d_ssm) // SSD_HEADDIM)[None, :]).astype(BF16)
    consts = [pool_w.astype(BF16), pool_scale.reshape(1, pool_dim), conv_w, conv_b.reshape(1, conv_dim),
              dtb, aneg, dsk, norm_g.reshape(1, d_ssm), rexp]
    widths = (pool_dim, d_ssm, conv_dim, LANES)
    max_win = max(POOL_WINDOWS)
    return pl.pallas_call(
        functools.partial(_mixer_kernel, pool_dim=pool_dim, d_ssm=d_ssm),
        grid=(bsz, seq // TM_MIX),
        in_specs=[pl.BlockSpec((None, TM_MIX, wd), lambda b, l: (b, l, 0)) for wd in widths]
        + [_const_spec(a.shape) for a in consts],
        out_specs=pl.BlockSpec((None, TM_MIX, pool_dim + d_ssm), lambda b, l: (b, l, 0)),
        out_shape=jax.ShapeDtypeStruct((bsz, seq, pool_dim + d_ssm), BF16),
        scratch_shapes=[
            pltpu.VMEM((SUBLANES + TM_MIX, conv_dim), F32),
            pltpu.VMEM((max_win + TM_MIX, pool_dim), F32),
            pltpu.VMEM((SSD_GROUPS, D_STATE, d_ssm // SSD_GROUPS), F32),
            pltpu.VMEM((TM_MIX, conv_dim), F32),
        ],
        compiler_params=_params("arbitrary", "arbitrary"),
        name="pool_ssd_core",
    )(u, z, xbc, dtr, *consts)


def _mla_proj_kernel(x_ref, mod_ref, g_ref, pos_ref, win_ref, qg_ref, wuq_ref, kvg_ref, wk_ref,
                     wv_ref, invf_ref, q_ref, k_ref, v_ref, *, q_lora, kv_lora, scale):
    h = _norm_mod(x_ref[...], g_ref[...], mod_ref[3:4, :], mod_ref[4:5, :]).astype(BF16)
    p = _dot(h, win_ref[...])
    q_a = p[:, 0:q_lora]
    kv_a = p[:, q_lora:q_lora + kv_lora]
    k_r = p[:, q_lora + kv_lora:q_lora + kv_lora + HEAD_PAD]

    def rms(t, gain):
        return t * lax.rsqrt(jnp.mean(t * t, axis=-1, keepdims=True) + RMS_EPS) * gain

    qn = rms(q_a, qg_ref[...]).astype(BF16)
    kvn = rms(kv_a, kvg_ref[...]).astype(BF16)

    ang = pos_ref[...] * invf_ref[...]
    cs = jnp.cos(ang)
    sn = jnp.sin(ang)
    lane = lax.broadcasted_iota(jnp.int32, (1, HEAD_PAD), 1)
    first = (lane >= ROPE_LANE0) & (lane < ROPE_LANE0 + HALF_ROPE)
    second = (lane >= ROPE_LANE0 + HALF_ROPE) & (lane < ROPE_LANE0 + QK_ROPE)
    s_first = jnp.where(first, -sn, 0.0)
    s_second = jnp.where(second, sn, 0.0)

    def rope(t, mult):
        up = pltpu.roll(t, HEAD_PAD - HALF_ROPE, 1)
        down = pltpu.roll(t, HALF_ROPE, 1)
        return (t * cs + up * s_first + down * s_second) * mult

    qf = _dot(qn, wuq_ref[...])
    kf = _dot(kvn, wk_ref[...])
    k_rot = rope(k_r, 1.0)
    n_heads = q_ref.shape[0]
    for hd in range(n_heads):
        cols = slice(hd * HEAD_PAD, (hd + 1) * HEAD_PAD)
        q_ref[hd] = rope(qf[:, cols], scale).astype(BF16)
        k_ref[hd] = (kf[:, cols] + k_rot).astype(BF16)
    vf = _dot(kvn, wv_ref[...])
    for hp in range(v_ref.shape[0]):
        v_ref[hp] = vf[:, hp * LANES:(hp + 1) * LANES].astype(BF16)


def _mla_proj(x, modl, g, positions, w_in, q_norm_g, w_uq, kv_norm_g, w_ukv):
    bsz, seq, d = x.shape
    q_lora = w_uq.shape[0]
    kv_lora = w_ukv.shape[0]
    nh = MLA_HEADS
    w_rope = jnp.pad(w_in[:, q_lora + kv_lora:], ((0, 0), (ROPE_LANE0, HEAD_PAD - ROPE_LANE0 - QK_ROPE)))
    win = jnp.concatenate([w_in[:, :q_lora + kv_lora], w_rope], axis=1).astype(BF16)
    wuq = jnp.pad(w_uq.reshape(q_lora, nh, QK_NOPE + QK_ROPE),
                  ((0, 0), (0, 0), (0, HEAD_PAD - QK_NOPE - QK_ROPE))).reshape(q_lora, nh * HEAD_PAD)
    wkv = w_ukv.reshape(kv_lora, nh, QK_NOPE + V_DIM)
    wk = jnp.pad(wkv[:, :, :QK_NOPE], ((0, 0), (0, 0), (0, HEAD_PAD - QK_NOPE))).reshape(kv_lora, nh * HEAD_PAD)
    wv = wkv[:, :, QK_NOPE:].reshape(kv_lora, nh * V_DIM)
    inv_freq = ROPE_THETA ** (-jnp.arange(0, QK_ROPE, 2, dtype=F32) / QK_ROPE)
    invf = jnp.zeros((HEAD_PAD,), F32).at[ROPE_LANE0:ROPE_LANE0 + QK_ROPE].set(jnp.tile(inv_freq, 2))
    pos = positions.astype(F32).reshape(bsz, seq, 1)
    scale = math.log2(math.e) / math.sqrt(QK_NOPE + QK_ROPE)
    consts = [win, q_norm_g.reshape(1, q_lora), wuq.astype(BF16), kv_norm_g.reshape(1, kv_lora),
              wk.astype(BF16), wv.astype(BF16), invf.reshape(1, HEAD_PAD)]
    n_vp = nh * V_DIM // LANES
    return pl.pallas_call(
        functools.partial(_mla_proj_kernel, q_lora=q_lora, kv_lora=kv_lora, scale=scale),
        grid=(bsz, seq // TM_PROJ),
        in_specs=[
            pl.BlockSpec((None, TM_PROJ, d), lambda b, l: (b, l, 0)),
            pl.BlockSpec((None, N_MOD, d), lambda b, l: (b, 0, 0)),
            _const_spec((1, d)),
            pl.BlockSpec((None, TM_PROJ, 1), lambda b, l: (b, l, 0)),
        ] + [_const_spec(a.shape) for a in consts],
        out_specs=[
            pl.BlockSpec((None, nh, TM_PROJ, HEAD_PAD), lambda b, l: (b, 0, l, 0)),
            pl.BlockSpec((None, nh, TM_PROJ, HEAD_PAD), lambda b, l: (b, 0, l, 0)),
            pl.BlockSpec((None, n_vp, TM_PROJ, LANES), lambda b, l: (b, 0, l, 0)),
        ],
        out_shape=[
            jax.ShapeDtypeStruct((bsz, nh, seq, HEAD_PAD), BF16),
            jax.ShapeDtypeStruct((bsz, nh, seq, HEAD_PAD), BF16),
            jax.ShapeDtypeStruct((bsz, n_vp, seq, LANES), BF16),
        ],
        compiler_params=_params("arbitrary", "arbitrary"),
        name="mla_qkv_proj",
    )(x, modl, g.reshape(1, d), pos, *consts)


def _attn_kernel(q_ref, k_ref, v_ref, o_ref):
    seq = v_ref.shape[0]
    lane = lax.broadcasted_iota(jnp.int32, (1, LANES), 1)
    lo_half = lane < V_DIM

    def q_tile(qi, carry):
        q0 = pl.multiple_of(qi * TQ, TQ)
        qs = [q_ref[hh, pl.ds(q0, TQ), :] for hh in range(2)]

        def block(state, k0, tk, r0, masked):
            v = v_ref[pl.ds(k0, tk), :]
            zero = jnp.zeros_like(v)
            v_heads = (jnp.where(lo_half, v, zero), jnp.where(lo_half, zero, v))
            new = []
            alphas = []
            pv = None
            for hh in range(2):
                m_old, l_old = state[2 * hh][r0:], state[2 * hh + 1][r0:]
                s = _dot_nt(qs[hh][r0:], k_ref[hh, pl.ds(k0, tk), :])
                if masked:
                    row = lax.broadcasted_iota(jnp.int32, s.shape, 0) + (q0 + r0)
                    col = lax.broadcasted_iota(jnp.int32, s.shape, 1) + k0
                    s = jnp.where(row >= col, s, -jnp.inf)
                m_new = jnp.maximum(m_old, jnp.max(s, axis=-1, keepdims=True))
                alpha = jnp.exp2(m_old - m_new)
                p = jnp.exp2(s - m_new)
                new += [m_new, alpha * l_old + jnp.sum(p, axis=-1, keepdims=True)]
                alphas.append(alpha)
                d = _dot(p.astype(BF16), v_heads[hh])
                pv = d if pv is None else pv + d
            new.append(state[4][r0:] * jnp.where(lo_half, alphas[0], alphas[1]) + pv)
            if r0:
                new = [jnp.concatenate([old[:r0], upd], axis=0) for old, upd in zip(state, new)]
            return tuple(new)

        def full_step(j, state):
            return block(state, pl.multiple_of(j * TK, TK), TK, 0, False)

        init = (jnp.full((TQ, 1), -jnp.inf, F32), jnp.zeros((TQ, 1), F32),
                jnp.full((TQ, 1), -jnp.inf, F32), jnp.zeros((TQ, 1), F32),
                jnp.zeros((TQ, LANES), F32))
        state = lax.fori_loop(0, qi * (TQ // TK), full_step, init)
        for d in range(TQ // TD):
            state = block(state, q0 + d * TD, TD, d * TD, True)
        o_ref[pl.ds(q0, TQ), :] = (state[4] / jnp.where(lo_half, state[1], state[3])).astype(BF16)
        return carry

    lax.fori_loop(0, seq // TQ, q_tile, 0)


def _attention(q, k, v):
    bsz, nh, seq, _ = q.shape
    n_pairs = nh // 2
    return pl.pallas_call(
        _attn_kernel,
        grid=(bsz, n_pairs),
        in_specs=[
            pl.BlockSpec((None, 2, seq, HEAD_PAD), lambda b, hp: (b, hp, 0, 0)),
            pl.BlockSpec((None, 2, seq, HEAD_PAD), lambda b, hp: (b, hp, 0, 0)),
            pl.BlockSpec((None, None, seq, LANES), lambda b, hp: (b, hp, 0, 0)),
        ],
        out_specs=pl.BlockSpec((None, seq, LANES), lambda b, hp: (b, 0, hp)),
        out_shape=jax.ShapeDtypeStruct((bsz, seq, nh * V_DIM), BF16),
        compiler_params=_params("arbitrary", "arbitrary"),
        name="mla_causal_attention",
    )(q, k, v)


def kernel(x, c, positions, mod_w, mod_b, norm_g, ffn_w13, ffn_w2, ab_w_in, pool_w, pool_scale, ssd_conv_w, ssd_conv_b, ssd_dt_bias, ssd_a_log, ssd_d, ssd_norm_g, ab_w_out, mla_w_in, mla_q_norm_g, mla_w_uq, mla_kv_norm_g, mla_w_ukv, mla_w_o, final_norm_g):
    depth = mod_w.shape[0]
    mod = _modulation(c, mod_w, mod_b)
    for i in range(depth):
        modl = mod[i]
        j = i // 2
        x = _ffn(x, modl, norm_g[i, 0], ffn_w13[i, 0], ffn_w2[i, 0], sub=0)
        if i % 2 == 0:
            pool_dim = pool_scale.shape[-1]
            d_ssm = ssd_norm_g.shape[-1]
            conv_dim = ssd_conv_b.shape[-1]
            n_heads = ssd_dt_bias.shape[-1]
            u, z, xbc, dtr = _ab_in_proj(x, modl, norm_g[i, 1], ab_w_in[j], pool_dim, d_ssm, conv_dim, n_heads)
            y = _mixer_core(u, z, xbc, dtr, pool_w[j], pool_scale[j], ssd_conv_w[j], ssd_conv_b[j],
                            ssd_dt_bias[j], ssd_a_log[j], ssd_d[j], ssd_norm_g[j])
            x = _out_proj(x, y, modl, ab_w_out[j])
        else:
            q, k, v = _mla_proj(x, modl, norm_g[i, 1], positions, mla_w_in[j], mla_q_norm_g[j],
                                mla_w_uq[j], mla_kv_norm_g[j], mla_w_ukv[j])
            o = _attention(q, k, v)
            x = _out_proj(x, o, modl, mla_w_o[j])
        last = i == depth - 1
        x = _ffn(x, modl, norm_g[i, 2], ffn_w13[i, 1], ffn_w2[i, 1], sub=2,
                 final_g=final_norm_g if last else None)
    return x
```

```python
import functools
import math

import jax
import jax.numpy as jnp
from jax import lax
from jax.experimental import pallas as pl
from jax.experimental.pallas import tpu as pltpu

F32 = jnp.float32
BF16 = jnp.bfloat16

RMS_EPS = 1e-6
FFN_RES = 0.5
POOL_WINDOWS = (2, 4, 8, 16)
SSD_HEADDIM = 64
SSD_GROUPS = 2
D_STATE = 128
CONV_K = 4
SSD_CHUNK = 128
MLA_HEADS = 16
QK_NOPE = 64
QK_ROPE = 32
V_DIM = 64
ROPE_THETA = 10000.0
N_MOD = 9

LANES = 128
SUBLANES = 8
BF16_SUBLANES = 16
VMEM_LIMIT_BYTES = 56 * 1024 * 1024

TM_FFN = 512
FFN_CHUNK = 256
TM_PROJ = 512
TM_MIX = 256
TM_MOD = 1024
TQ = 512
TK = 256
HALF_ROPE = QK_ROPE // 2
HEAD_PAD = 128
ROPE_LANE0 = QK_NOPE


def _params(*sem, flags=None):
    return pltpu.CompilerParams(dimension_semantics=sem, vmem_limit_bytes=VMEM_LIMIT_BYTES, flags=flags)


def _const_spec(shape):
    nd = len(shape)
    return pl.BlockSpec(shape, lambda *_: (0,) * nd, pipeline_mode=pl.Buffered(1))


def _silu(a):
    return a * jax.nn.sigmoid(a)


def _dot(a, b):
    return jnp.dot(a, b, preferred_element_type=F32)


def _dot_nt(a, b):
    return lax.dot_general(a, b, (((1,), (1,)), ((), ())), preferred_element_type=F32)


def _norm_mod(x, g, shift, scale):
    ms = jnp.mean(x * x, axis=-1, keepdims=True)
    y = x * lax.rsqrt(ms + RMS_EPS) * g
    return y * (1.0 + scale) + shift


def _split_bf16(x, parts):
    out = []
    r = x
    for _ in range(parts):
        p = r.astype(BF16)
        out.append(p)
        r = r - p.astype(F32)
    return out


def _mod_kernel(c_ref, w_ref, b_ref, o_ref):
    ca = _split_bf16(_silu(c_ref[...]), 3)
    w = _split_bf16(w_ref[...], 3)
    acc = b_ref[...]
    for i, j in ((2, 0), (1, 1), (0, 2), (1, 0), (0, 1), (0, 0)):
        acc = acc + _dot(ca[i], w[j])
    o_ref[...] = acc


def _modulation(c, mod_w, mod_b):
    depth, d, n = mod_w.shape
    bsz = c.shape[0]
    out = pl.pallas_call(
        _mod_kernel,
        grid=(depth, n // TM_MOD),
        in_specs=[
            pl.BlockSpec((bsz, d), lambda i, j: (0, 0)),
            pl.BlockSpec((None, d, TM_MOD), lambda i, j: (i, 0, j)),
            pl.BlockSpec((None, 1, TM_MOD), lambda i, j: (i, 0, j)),
        ],
        out_specs=pl.BlockSpec((None, bsz, TM_MOD), lambda i, j: (i, 0, j)),
        out_shape=jax.ShapeDtypeStruct((depth, bsz, n), F32),
        compiler_params=_params("arbitrary", "arbitrary"),
        name="adaln_modulation",
    )(c, mod_w, mod_b.reshape(depth, 1, n))
    return out.reshape(depth, bsz, N_MOD, d)


def _ffn_kernel(*refs, sub, d_ff, final):
    if final:
        x_ref, mod_ref, g_ref, w13_ref, w2_ref, fg_ref, o_ref, h_ref, acc_ref = refs
    else:
        x_ref, mod_ref, g_ref, w13_ref, w2_ref, o_ref, h_ref, acc_ref = refs
    x = x_ref[...]
    shift = mod_ref[3 * sub:3 * sub + 1, :]
    scale = mod_ref[3 * sub + 1:3 * sub + 2, :]
    gate = mod_ref[3 * sub + 2:3 * sub + 3, :]
    h_ref[...] = _norm_mod(x, g_ref[...], shift, scale).astype(BF16)
    for c0 in range(0, d_ff, FFN_CHUNK):
        h = h_ref[...]
        a = _dot(h, w13_ref[:, c0:c0 + FFN_CHUNK])
        b = _dot(h, w13_ref[:, d_ff + c0:d_ff + c0 + FFN_CHUNK])
        part = _dot((_silu(a) * b).astype(BF16), w2_ref[c0:c0 + FFN_CHUNK, :])
        if c0 == 0:
            acc_ref[...] = part
        else:
            acc_ref[...] += part
    y = x + (FFN_RES * gate) * acc_ref[...]
    if final:
        ms = jnp.mean(y * y, axis=-1, keepdims=True)
        y = y * lax.rsqrt(ms + RMS_EPS) * fg_ref[...]
    o_ref[...] = y


def _ffn(x, modl, g, w13, w2, sub, final_g=None):
    bsz, seq, d = x.shape
    d_ff = w2.shape[0]
    assert d_ff % FFN_CHUNK == 0
    final = final_g is not None
    in_specs = [
        pl.BlockSpec((None, TM_FFN, d), lambda b, l: (b, l, 0)),
        pl.BlockSpec((None, N_MOD, d), lambda b, l: (b, 0, 0)),
        _const_spec((1, d)),
        _const_spec((d, 2 * d_ff)),
        _const_spec((d_ff, d)),
    ]
    args = [x, modl, g.reshape(1, d), w13.astype(BF16), w2.astype(BF16)]
    if final:
        in_specs.append(_const_spec((1, d)))
        args.append(final_g.reshape(1, d))
    return pl.pallas_call(
        functools.partial(_ffn_kernel, sub=sub, d_ff=d_ff, final=final),
        grid=(bsz, seq // TM_FFN),
        in_specs=in_specs,
        out_specs=pl.BlockSpec((None, TM_FFN, d), lambda b, l: (b, l, 0)),
        out_shape=jax.ShapeDtypeStruct((bsz, seq, d), F32),
        scratch_shapes=[pltpu.VMEM((TM_FFN, d), BF16), pltpu.VMEM((TM_FFN, d), F32)],
        compiler_params=_params("arbitrary", "arbitrary"),
        name="swiglu_ffn_final" if final else "swiglu_ffn",
    )(*args)


def _out_proj_kernel(x_ref, y_ref, mod_ref, w_ref, o_ref):
    gate = mod_ref[5:6, :]
    o_ref[...] = x_ref[...] + gate * _dot(y_ref[...], w_ref[...])


def _out_proj(x, y, modl, w):
    bsz, seq, d = x.shape
    k = y.shape[-1]
    return pl.pallas_call(
        _out_proj_kernel,
        grid=(bsz, seq // TM_PROJ),
        in_specs=[
            pl.BlockSpec((None, TM_PROJ, d), lambda b, l: (b, l, 0)),
            pl.BlockSpec((None, TM_PROJ, k), lambda b, l: (b, l, 0)),
            pl.BlockSpec((None, N_MOD, d), lambda b, l: (b, 0, 0)),
            _const_spec((k, d)),
        ],
        out_specs=pl.BlockSpec((None, TM_PROJ, d), lambda b, l: (b, l, 0)),
        out_shape=jax.ShapeDtypeStruct((bsz, seq, d), F32),
        compiler_params=_params("arbitrary", "arbitrary"),
        name="mixer_out_proj",
    )(x, y, modl, w.astype(BF16))


def _ab_in_kernel(x_ref, mod_ref, g_ref, w_ref, u_ref, z_ref, xbc_ref, dt_ref, *, widths):
    h = _norm_mod(x_ref[...], g_ref[...], mod_ref[3:4, :], mod_ref[4:5, :]).astype(BF16)
    p = _dot(h, w_ref[...])
    c0 = 0
    for ref, wd in zip((u_ref, z_ref, xbc_ref, dt_ref), widths):
        ref[...] = p[:, c0:c0 + wd]
        c0 += wd


def _ab_in_proj(x, modl, g, w_in, pool_dim, d_ssm, conv_dim, n_heads):
    bsz, seq, d = x.shape
    main = pool_dim + d_ssm + conv_dim
    w_dt = jnp.pad(w_in[:, main:main + n_heads], ((0, 0), (0, LANES - n_heads)))
    w = jnp.concatenate([w_in[:, :main], w_dt], axis=1).astype(BF16)
    widths = (pool_dim, d_ssm, conv_dim, LANES)
    return pl.pallas_call(
        functools.partial(_ab_in_kernel, widths=widths),
        grid=(bsz, seq // TM_PROJ),
        in_specs=[
            pl.BlockSpec((None, TM_PROJ, d), lambda b, l: (b, l, 0)),
            pl.BlockSpec((None, N_MOD, d), lambda b, l: (b, 0, 0)),
            _const_spec((1, d)),
            _const_spec(w.shape),
        ],
        out_specs=[pl.BlockSpec((None, TM_PROJ, wd), lambda b, l: (b, l, 0)) for wd in widths],
        out_shape=[jax.ShapeDtypeStruct((bsz, seq, wd), F32) for wd in widths],
        compiler_params=_params("arbitrary", "arbitrary"),
        name="pool_ssd_in_proj",
    )(x, modl, g.reshape(1, d), w)


def _softplus(x):
    return jnp.maximum(x, 0.0) + jnp.log1p(jnp.exp(-jnp.abs(x)))


def _mixer_kernel(u_ref, z_ref, xbc_ref, dtr_ref, pw_ref, ps_ref, cw_ref, cb_ref, dtb_ref,
                  aneg_ref, dsk_ref, ng_ref, rexp_ref, o_ref, xext, uext, st_ref, xc_ref,
                  *, pool_dim, d_ssm):
    tm = u_ref.shape[0]
    q = SSD_CHUNK
    n_groups = SSD_GROUPS
    gdim = d_ssm // n_groups
    heads_per_group = gdim // SSD_HEADDIM
    pool_gdim = pool_dim // len(POOL_WINDOWS)
    max_win = max(POOL_WINDOWS)
    lt = pl.program_id(1)

    @pl.when(lt == 0)
    def _():
        xext[0:SUBLANES, :] = jnp.zeros((SUBLANES, xext.shape[1]), F32)
        uext[0:max_win, :] = jnp.zeros((max_win, uext.shape[1]), F32)
        st_ref[...] = jnp.zeros_like(st_ref)

    xext[SUBLANES:SUBLANES + tm, :] = xbc_ref[...]
    acc = cb_ref[...] + cw_ref[CONV_K - 1:CONV_K, :] * xbc_ref[...]
    for k in range(CONV_K - 1):
        off = SUBLANES - (CONV_K - 1) + k
        acc = acc + cw_ref[k:k + 1, :] * xext[off:off + tm, :]
    xc_ref[...] = _silu(acc)
    xext[0:SUBLANES, :] = xext[tm:tm + SUBLANES, :]

    uext[max_win:max_win + tm, :] = u_ref[...]
    tpos = (lt * tm + lax.broadcasted_iota(jnp.int32, (tm, pool_gdim), 0)).astype(F32)
    for gi, win in enumerate(POOL_WINDOWS):
        cols = slice(gi * pool_gdim, (gi + 1) * pool_gdim)
        cur = u_ref[:, cols]
        ws = cur
        for k in range(1, win):
            ws = ws + uext[max_win - k:max_win - k + tm, cols]
        count = jnp.minimum(tpos + 1.0, float(win))
        dlt = ws / count - cur
        yp = _dot(dlt.astype(BF16), pw_ref[gi]) * ps_ref[:, cols]
        o_ref[:, cols] = yp.astype(BF16)
    uext[0:max_win, :] = uext[tm:tm + max_win, :]

    row = lax.broadcasted_iota(jnp.int32, (q, q), 0)
    col = lax.broadcasted_iota(jnp.int32, (q, q), 1)
    causal = row >= col
    tril = jnp.where(causal, 1.0, 0.0).astype(BF16)
    lane = lax.broadcasted_iota(jnp.int32, (1, LANES), 1)
    lo_half = lane < SSD_HEADDIM
    rexp = rexp_ref[...]

    def expand(v):
        hi, lo = _split_bf16(v, 2)
        return _dot(hi, rexp) + _dot(lo, rexp)

    for ci in range(tm // q):
        r0 = ci * q
        xs = xc_ref[r0:r0 + q, 0:d_ssm]
        dt = _softplus(dtr_ref[r0:r0 + q, :] + dtb_ref[...])
        adt = dt * aneg_ref[...]
        acs = sum(_dot(tril, p) for p in _split_bf16(adt, 3))
        acs_t = acs.T
        a_last = acs[q - 1:q, :]
        dt_e = expand(dt)
        eacs_e = expand(jnp.exp(acs))
        wend_e = expand(jnp.exp(a_last - acs) * dt)
        xdt = xs * dt_e
        xw = xs * wend_e
        for g in range(n_groups):
            gc = slice(g * gdim, (g + 1) * gdim)
            b0 = d_ssm + g * D_STATE
            c0 = d_ssm + n_groups * D_STATE + g * D_STATE
            bm = xc_ref[r0:r0 + q, b0:b0 + D_STATE]
            cm = xc_ref[r0:r0 + q, c0:c0 + D_STATE].astype(BF16)
            cb = _dot_nt(cm, bm.astype(BF16))
            s_prev = st_ref[g]
            y_off = _dot(cm, s_prev.astype(BF16)) * eacs_e[:, gc]
            ys = []
            for pr in range(heads_per_group // 2):
                h0 = g * heads_per_group + 2 * pr
                gs = []
                for hh in range(2):
                    h = h0 + hh
                    diff = acs[:, h:h + 1] - acs_t[h:h + 1, :]
                    dec = jnp.exp(jnp.where(causal, diff, -jnp.inf))
                    gs.append((cb * dec).astype(BF16))
                xp = xdt[:, h0 * SSD_HEADDIM:(h0 + 2) * SSD_HEADDIM]
                xbd = jnp.concatenate([jnp.where(lo_half, xp, 0.0), jnp.where(lo_half, 0.0, xp)],
                                      axis=0).astype(BF16)
                ys.append(_dot(jnp.concatenate(gs, axis=1), xbd))
            y = jnp.concatenate(ys, axis=1) + y_off
            st_ref[g] = eacs_e[q - 1:q, gc] * s_prev + _dot(bm.T.astype(BF16), xw[:, gc].astype(BF16))
            y = y + dsk_ref[:, gc] * xs[:, gc]
            y = y * _silu(z_ref[r0:r0 + q, gc])
            ms = jnp.mean(y * y, axis=-1, keepdims=True)
            y = y * lax.rsqrt(ms + RMS_EPS) * ng_ref[:, gc]
            o_ref[r0:r0 + q, pool_dim + g * gdim:pool_dim + (g + 1) * gdim] = y.astype(BF16)


def _mixer_core(u, z, xbc, dtr, pool_w, pool_scale, conv_w, conv_b, dt_bias, a_log, d_skip, norm_g):
    bsz, seq, pool_dim = u.shape
    d_ssm = z.shape[-1]
    conv_dim = xbc.shape[-1]
    n_heads = dt_bias.shape[0]
    pad = LANES - n_heads
    dtb = jnp.pad(dt_bias.astype(F32), (0, pad)).reshape(1, LANES)
    aneg = jnp.pad(-jnp.exp(a_log.astype(F32)), (0, pad)).reshape(1, LANES)
    dsk = jnp.repeat(d_skip.astype(F32), SSD_HEADDIM).reshape(1, d_ssm)
    rexp = (jnp.arange(LANES)[:, None] == (jnp.arange(d_ssm) // SSD_HEADDIM)[None, :]).astype(BF16)
    consts = [pool_w.astype(BF16), pool_scale.reshape(1, pool_dim), conv_w, conv_b.reshape(1, conv_dim),
              dtb, aneg, dsk, norm_g.reshape(1, d_ssm), rexp]
    widths = (pool_dim, d_ssm, conv_dim, LANES)
    max_win = max(POOL_WINDOWS)
    return pl.pallas_call(
        functools.partial(_mixer_kernel, pool_dim=pool_dim, d_ssm=d_ssm),
        grid=(bsz, seq // TM_MIX),
        in_specs=[pl.BlockSpec((None, TM_MIX, wd), lambda b, l: (b, l, 0)) for wd in widths]
        + [_const_spec(a.shape) for a in consts],
        out_specs=pl.BlockSpec((None, TM_MIX, pool_dim + d_ssm), lambda b, l: (b, l, 0)),
        out_shape=jax.ShapeDtypeStruct((bsz, seq, pool_dim + d_ssm), BF16),
        scratch_shapes=[
            pltpu.VMEM((SUBLANES + TM_MIX, conv_dim), F32),
            pltpu.VMEM((max_win + TM_MIX, pool_dim), F32),
            pltpu.VMEM((SSD_GROUPS, D_STATE, d_ssm // SSD_GROUPS), F32),
            pltpu.VMEM((TM_MIX, conv_dim), F32),
        ],
        compiler_params=_params("arbitrary", "arbitrary"),
        name="pool_ssd_core",
    )(u, z, xbc, dtr, *consts)


def _mla_proj_kernel(x_ref, mod_ref, g_ref, pos_ref, win_ref, qg_ref, wuq_ref, kvg_ref, wk_ref,
                     wv_ref, invf_ref, q_ref, k_ref, v_ref, *, q_lora, kv_lora, scale):
    h = _norm_mod(x_ref[...], g_ref[...], mod_ref[3:4, :], mod_ref[4:5, :]).astype(BF16)
    p = _dot(h, win_ref[...])
    q_a = p[:, 0:q_lora]
    kv_a = p[:, q_lora:q_lora + kv_lora]
    k_r = p[:, q_lora + kv_lora:q_lora + kv_lora + HEAD_PAD]

    def rms(t, gain):
        return t * lax.rsqrt(jnp.mean(t * t, axis=-1, keepdims=True) + RMS_EPS) * gain

    qn = rms(q_a, qg_ref[...]).astype(BF16)
    kvn = rms(kv_a, kvg_ref[...]).astype(BF16)

    ang = pos_ref[...] * invf_ref[...]
    cs = jnp.cos(ang)
    sn = jnp.sin(ang)
    lane = lax.broadcasted_iota(jnp.int32, (1, HEAD_PAD), 1)
    first = (lane >= ROPE_LANE0) & (lane < ROPE_LANE0 + HALF_ROPE)
    second = (lane >= ROPE_LANE0 + HALF_ROPE) & (lane < ROPE_LANE0 + QK_ROPE)
    s_first = jnp.where(first, -sn, 0.0)
    s_second = jnp.where(second, sn, 0.0)

    def rope(t, mult):
        up = pltpu.roll(t, HEAD_PAD - HALF_ROPE, 1)
        down = pltpu.roll(t, HALF_ROPE, 1)
        return (t * cs + up * s_first + down * s_second) * mult

    qf = _dot(qn, wuq_ref[...])
    kf = _dot(kvn, wk_ref[...])
    k_rot = rope(k_r, 1.0)
    n_heads = q_ref.shape[0]
    for hd in range(n_heads):
        cols = slice(hd * HEAD_PAD, (hd + 1) * HEAD_PAD)
        q_ref[hd] = rope(qf[:, cols], scale).astype(BF16)
        k_ref[hd] = (kf[:, cols] + k_rot).astype(BF16)
    vt = _dot_nt(wv_ref[...], kvn)
    for hp in range(v_ref.shape[0]):
        v_ref[hp] = vt[hp * LANES:(hp + 1) * LANES, :].astype(BF16)


def _mla_proj(x, modl, g, positions, w_in, q_norm_g, w_uq, kv_norm_g, w_ukv):
    bsz, seq, d = x.shape
    q_lora = w_uq.shape[0]
    kv_lora = w_ukv.shape[0]
    nh = MLA_HEADS
    w_rope = jnp.pad(w_in[:, q_lora + kv_lora:], ((0, 0), (ROPE_LANE0, HEAD_PAD - ROPE_LANE0 - QK_ROPE)))
    win = jnp.concatenate([w_in[:, :q_lora + kv_lora], w_rope], axis=1).astype(BF16)
    wuq = jnp.pad(w_uq.reshape(q_lora, nh, QK_NOPE + QK_ROPE),
                  ((0, 0), (0, 0), (0, HEAD_PAD - QK_NOPE - QK_ROPE))).reshape(q_lora, nh * HEAD_PAD)
    wkv = w_ukv.reshape(kv_lora, nh, QK_NOPE + V_DIM)
    wk = jnp.pad(wkv[:, :, :QK_NOPE], ((0, 0), (0, 0), (0, HEAD_PAD - QK_NOPE))).reshape(kv_lora, nh * HEAD_PAD)
    wv = wkv[:, :, QK_NOPE:].reshape(kv_lora, nh * V_DIM).T
    inv_freq = ROPE_THETA ** (-jnp.arange(0, QK_ROPE, 2, dtype=F32) / QK_ROPE)
    invf = jnp.zeros((HEAD_PAD,), F32).at[ROPE_LANE0:ROPE_LANE0 + QK_ROPE].set(jnp.tile(inv_freq, 2))
    pos = positions.astype(F32).reshape(bsz, seq, 1)
    scale = math.log2(math.e) / math.sqrt(QK_NOPE + QK_ROPE)
    consts = [win, q_norm_g.reshape(1, q_lora), wuq.astype(BF16), kv_norm_g.reshape(1, kv_lora),
              wk.astype(BF16), wv.astype(BF16), invf.reshape(1, HEAD_PAD)]
    n_vp = nh * V_DIM // LANES
    return pl.pallas_call(
        functools.partial(_mla_proj_kernel, q_lora=q_lora, kv_lora=kv_lora, scale=scale),
        grid=(bsz, seq // TM_PROJ),
        in_specs=[
            pl.BlockSpec((None, TM_PROJ, d), lambda b, l: (b, l, 0)),
            pl.BlockSpec((None, N_MOD, d), lambda b, l: (b, 0, 0)),
            _const_spec((1, d)),
            pl.BlockSpec((None, TM_PROJ, 1), lambda b, l: (b, l, 0)),
        ] + [_const_spec(a.shape) for a in consts],
        out_specs=[
            pl.BlockSpec((None, nh, TM_PROJ, HEAD_PAD), lambda b, l: (b, 0, l, 0)),
            pl.BlockSpec((None, nh, TM_PROJ, HEAD_PAD), lambda b, l: (b, 0, l, 0)),
            pl.BlockSpec((None, n_vp, LANES, TM_PROJ), lambda b, l: (b, 0, 0, l)),
        ],
        out_shape=[
            jax.ShapeDtypeStruct((bsz, nh, seq, HEAD_PAD), BF16),
            jax.ShapeDtypeStruct((bsz, nh, seq, HEAD_PAD), BF16),
            jax.ShapeDtypeStruct((bsz, n_vp, LANES, seq), BF16),
        ],
        compiler_params=_params("arbitrary", "arbitrary"),
        name="mla_qkv_proj",
    )(x, modl, g.reshape(1, d), pos, *consts)


def _attn_kernel(q_ref, k_ref, vt_ref, o_ref, sa_ref, sb_ref, pa_ref, pb_ref):
    seq = vt_ref.shape[1]
    assert TQ == 2 * TK
    n_groups = TQ // LANES
    diag_groups = TK // LANES
    row_minus_lane = (lax.broadcasted_iota(jnp.int32, (TK, LANES), 0)
                      - lax.broadcasted_iota(jnp.int32, (TK, LANES), 1))
    ones_rows = jnp.ones((BF16_SUBLANES, TK), BF16)

    def load_q(q0):
        return [q_ref[hh, pl.ds(q0, TQ), :] for hh in range(2)]

    def scores(dst_ref, qs, k0):
        for hh in range(2):
            dst_ref[hh] = _dot_nt(k_ref[hh, pl.ds(k0, TK), :], qs[hh])

    def softmax(state, s_ref, p_ref, groups):
        out = []
        for hh in range(2):
            m_old = state[3 * hh]
            ms, alphas = [], []
            for grp, delta in groups:
                cols = slice(grp * LANES, (grp + 1) * LANES)
                st = s_ref[hh, :, cols]
                if delta is not None:
                    st = jnp.where(row_minus_lane <= delta, st, -jnp.inf)
                m_prev = m_old[:, cols]
                m_new = jnp.maximum(m_prev, jnp.max(st, axis=0, keepdims=True))
                ms.append(m_new)
                alphas.append(jnp.exp2(m_prev - m_new))
                p_ref[hh, :, cols] = jnp.exp2(st - m_new).astype(BF16)
            out.append((jnp.concatenate(ms, axis=1), jnp.concatenate(alphas, axis=1)))
        return out

    def accumulate(state, soft, p_ref, k0, lo):
        vt = vt_ref[:, pl.ds(k0, TK)]
        new = []
        for hh in range(2):
            m_old, l_old, acc = state[3 * hh:3 * hh + 3]
            m_new, alpha = soft[hh]
            lhs = jnp.concatenate([vt[hh * V_DIM:(hh + 1) * V_DIM, :], ones_rows], axis=0)
            pv = _dot(lhs, p_ref[hh, :, lo:])
            upd = [m_new, alpha * l_old[:, lo:] + pv[V_DIM:V_DIM + 1, :], acc[:, lo:] * alpha + pv[:V_DIM, :]]
            if lo:
                upd = [jnp.concatenate([old[:, :lo], u], axis=1) for old, u in zip((m_old, l_old, acc), upd)]
            new += upd
        return tuple(new)

    below = [(grp, None) for grp in range(n_groups)]
    diag0 = [(grp, grp * LANES if grp < diag_groups else None) for grp in range(n_groups)]
    diag1 = [(grp, (grp - diag_groups) * LANES) for grp in range(diag_groups, n_groups)]

    def q_tile(qi, carry):
        q0 = pl.multiple_of(qi * TQ, TQ)
        qs = load_q(q0)

        def pair(t, state):
            k0 = pl.multiple_of(2 * t * TK, 2 * TK)
            soft = softmax(state, sa_ref, pa_ref, below)
            scores(sa_ref, qs, k0 + 2 * TK)
            state = accumulate(state, soft, pa_ref, k0, 0)
            soft = softmax(state, sb_ref, pb_ref, below)
            scores(sb_ref, qs, k0 + 3 * TK)
            return accumulate(state, soft, pb_ref, k0 + TK, 0)

        init = (jnp.full((1, TQ), -jnp.inf, F32), jnp.zeros((1, TQ), F32),
                jnp.zeros((V_DIM, TQ), F32)) * 2
        state = lax.fori_loop(0, qi, pair, init)
        next_qs = load_q(pl.multiple_of(jnp.minimum(q0 + TQ, seq - TQ), TQ))
        soft = softmax(state, sa_ref, pa_ref, diag0)
        scores(sa_ref, next_qs, 0)
        state = accumulate(state, soft, pa_ref, q0, 0)
        soft = softmax(state, sb_ref, pb_ref, diag1)
        scores(sb_ref, next_qs, TK)
        state = accumulate(state, soft, pb_ref, q0 + TK, diag_groups * LANES)
        out_t = jnp.concatenate([state[2] / state[1], state[5] / state[4]], axis=0)
        o_ref[pl.ds(q0, TQ), :] = out_t.T.astype(BF16)
        return carry

    first_qs = load_q(0)
    scores(sa_ref, first_qs, 0)
    scores(sb_ref, first_qs, TK)
    lax.fori_loop(0, seq // TQ, q_tile, 0)


def _attention(q, k, v):
    bsz, nh, seq, _ = q.shape
    n_pairs = nh // 2
    return pl.pallas_call(
        _attn_kernel,
        grid=(bsz, n_pairs),
        in_specs=[
            pl.BlockSpec((None, 2, seq, HEAD_PAD), lambda b, hp: (b, hp, 0, 0)),
            pl.BlockSpec((None, 2, seq, HEAD_PAD), lambda b, hp: (b, hp, 0, 0)),
            pl.BlockSpec((None, None, LANES, seq), lambda b, hp: (b, hp, 0, 0)),
        ],
        out_specs=pl.BlockSpec((None, seq, LANES), lambda b, hp: (b, 0, hp)),
        out_shape=jax.ShapeDtypeStruct((bsz, seq, nh * V_DIM), BF16),
        scratch_shapes=[pltpu.VMEM((2, TK, TQ), F32), pltpu.VMEM((2, TK, TQ), F32),
                        pltpu.VMEM((2, TK, TQ), BF16), pltpu.VMEM((2, TK, TQ), BF16)],
        compiler_params=_params("arbitrary", "arbitrary"),
        name="mla_causal_attention",
    )(q, k, v)


def kernel(x, c, positions, mod_w, mod_b, norm_g, ffn_w13, ffn_w2, ab_w_in, pool_w, pool_scale, ssd_conv_w, ssd_conv_b, ssd_dt_bias, ssd_a_log, ssd_d, ssd_norm_g, ab_w_out, mla_w_in, mla_q_norm_g, mla_w_uq, mla_kv_norm_g, mla_w_ukv, mla_w_o, final_norm_g):
    depth = mod_w.shape[0]
    mod = _modulation(c, mod_w, mod_b)
    for i in range(depth):
        modl = mod[i]
        j = i // 2
        x = _ffn(x, modl, norm_g[i, 0], ffn_w13[i, 0], ffn_w2[i, 0], sub=0)
        if i % 2 == 0:
            pool_dim = pool_scale.shape[-1]
            d_ssm = ssd_norm_g.shape[-1]
            conv_dim = ssd_conv_b.shape[-1]
            n_heads = ssd_dt_bias.shape[-1]
            u, z, xbc, dtr = _ab_in_proj(x, modl, norm_g[i, 1], ab_w_in[j], pool_dim, d_ssm, conv_dim, n_heads)
            y = _mixer_core(u, z, xbc, dtr, pool_w[j], pool_scale[j], ssd_conv_w[j], ssd_conv_b[j],
                            ssd_dt_bias[j], ssd_a_log[j], ssd_d[j], ssd_norm_g[j])
            x = _out_proj(x, y, modl, ab_w_out[j])
        else:
            q, k, v = _mla_proj(x, modl, norm_g[i, 1], positions, mla_w_in[j], mla_q_norm_g[j],
                                mla_w_uq[j], mla_kv_norm_g[j], mla_w_ukv[j])
            o = _attention(q, k, v)
            x = _out_proj(x, o, modl, mla_w_o[j])
        last = i == depth - 1
        x = _ffn(x, modl, norm_g[i, 2], ffn_w13[i, 1], ffn_w2[i, 1], sub=2,
                 final_g=final_norm_g if last else None)
    return x
```

```python
import functools
import math

import jax
import jax.numpy as jnp
from jax import lax
from jax.experimental import pallas as pl
from jax.experimental.pallas import tpu as pltpu

F32 = jnp.float32
BF16 = jnp.bfloat16

RMS_EPS = 1e-6
FFN_RES = 0.5
POOL_WINDOWS = (2, 4, 8, 16)
SSD_HEADDIM = 64
SSD_GROUPS = 2
D_STATE = 128
CONV_K = 4
SSD_CHUNK = 128
MLA_HEADS = 16
QK_NOPE = 64
QK_ROPE = 32
V_DIM = 64
ROPE_THETA = 10000.0
N_MOD = 9

LANES = 128
SUBLANES = 8
BF16_SUBLANES = 16
VMEM_LIMIT_BYTES = 56 * 1024 * 1024

TM_FFN = 512
FFN_CHUNK = 256
TM_PROJ = 512
TM_MIX = 256
TM_MOD = 1024
TQ = 512
TK = 256
ATTN_HEADS = 4
HALF_ROPE = QK_ROPE // 2
HEAD_PAD = 128
ROPE_LANE0 = QK_NOPE


def _params(*sem, flags=None):
    return pltpu.CompilerParams(dimension_semantics=sem, vmem_limit_bytes=VMEM_LIMIT_BYTES, flags=flags)


def _const_spec(shape):
    nd = len(shape)
    return pl.BlockSpec(shape, lambda *_: (0,) * nd, pipeline_mode=pl.Buffered(1))


def _silu(a):
    return a * jax.nn.sigmoid(a)


def _dot(a, b):
    return jnp.dot(a, b, preferred_element_type=F32)


def _dot_nt(a, b):
    return lax.dot_general(a, b, (((1,), (1,)), ((), ())), preferred_element_type=F32)


def _norm_mod(x, g, shift, scale):
    ms = jnp.mean(x * x, axis=-1, keepdims=True)
    y = x * lax.rsqrt(ms + RMS_EPS) * g
    return y * (1.0 + scale) + shift


def _split_bf16(x, parts):
    out = []
    r = x
    for _ in range(parts):
        p = r.astype(BF16)
        out.append(p)
        r = r - p.astype(F32)
    return out


def _mod_kernel(c_ref, w_ref, b_ref, o_ref):
    ca = _split_bf16(_silu(c_ref[...]), 3)
    w = _split_bf16(w_ref[...], 3)
    acc = b_ref[...]
    for i, j in ((2, 0), (1, 1), (0, 2), (1, 0), (0, 1), (0, 0)):
        acc = acc + _dot(ca[i], w[j])
    o_ref[...] = acc


def _modulation(c, mod_w, mod_b):
    depth, d, n = mod_w.shape
    bsz = c.shape[0]
    out = pl.pallas_call(
        _mod_kernel,
        grid=(depth, n // TM_MOD),
        in_specs=[
            pl.BlockSpec((bsz, d), lambda i, j: (0, 0)),
            pl.BlockSpec((None, d, TM_MOD), lambda i, j: (i, 0, j)),
            pl.BlockSpec((None, 1, TM_MOD), lambda i, j: (i, 0, j)),
        ],
        out_specs=pl.BlockSpec((None, bsz, TM_MOD), lambda i, j: (i, 0, j)),
        out_shape=jax.ShapeDtypeStruct((depth, bsz, n), F32),
        compiler_params=_params("arbitrary", "arbitrary"),
        name="adaln_modulation",
    )(c, mod_w, mod_b.reshape(depth, 1, n))
    return out.reshape(depth, bsz, N_MOD, d)


def _ffn_kernel(*refs, sub, d_ff, final, mixed):
    x_ref, mod_ref, g_ref, w13_ref, w2_ref = refs[:5]
    o_ref, h_ref, acc_ref = refs[-3:]
    extra = refs[5:-3]
    x = x_ref[...]
    if mixed:
        y_ref, wo_ref = extra[:2]
        x = x + mod_ref[5:6, :] * _dot(y_ref[...], wo_ref[...])
    shift = mod_ref[3 * sub:3 * sub + 1, :]
    scale = mod_ref[3 * sub + 1:3 * sub + 2, :]
    gate = mod_ref[3 * sub + 2:3 * sub + 3, :]
    h_ref[...] = _norm_mod(x, g_ref[...], shift, scale).astype(BF16)
    for c0 in range(0, d_ff, FFN_CHUNK):
        h = h_ref[...]
        a = _dot(h, w13_ref[:, c0:c0 + FFN_CHUNK])
        b = _dot(h, w13_ref[:, d_ff + c0:d_ff + c0 + FFN_CHUNK])
        part = _dot((_silu(a) * b).astype(BF16), w2_ref[c0:c0 + FFN_CHUNK, :])
        if c0 == 0:
            acc_ref[...] = part
        else:
            acc_ref[...] += part
    y = x + (FFN_RES * gate) * acc_ref[...]
    if final:
        fg_ref = extra[-1]
        ms = jnp.mean(y * y, axis=-1, keepdims=True)
        y = y * lax.rsqrt(ms + RMS_EPS) * fg_ref[...]
    o_ref[...] = y


def _ffn(x, modl, g, w13, w2, sub, mixer=None, final_g=None):
    bsz, seq, d = x.shape
    d_ff = w2.shape[0]
    assert d_ff % FFN_CHUNK == 0
    in_specs = [
        pl.BlockSpec((None, TM_FFN, d), lambda b, l: (b, l, 0)),
        pl.BlockSpec((None, N_MOD, d), lambda b, l: (b, 0, 0)),
        _const_spec((1, d)),
        _const_spec((d, 2 * d_ff)),
        _const_spec((d_ff, d)),
    ]
    args = [x, modl, g.reshape(1, d), w13.astype(BF16), w2.astype(BF16)]
    if mixer is not None:
        y, w_out = mixer
        k = y.shape[-1]
        in_specs += [pl.BlockSpec((None, TM_FFN, k), lambda b, l: (b, l, 0)), _const_spec((k, d))]
        args += [y, w_out.astype(BF16)]
    if final_g is not None:
        in_specs.append(_const_spec((1, d)))
        args.append(final_g.reshape(1, d))
    return pl.pallas_call(
        functools.partial(_ffn_kernel, sub=sub, d_ff=d_ff, final=final_g is not None,
                          mixed=mixer is not None),
        grid=(bsz, seq // TM_FFN),
        in_specs=in_specs,
        out_specs=pl.BlockSpec((None, TM_FFN, d), lambda b, l: (b, l, 0)),
        out_shape=jax.ShapeDtypeStruct((bsz, seq, d), F32),
        scratch_shapes=[pltpu.VMEM((TM_FFN, d), BF16), pltpu.VMEM((TM_FFN, d), F32)],
        compiler_params=_params("arbitrary", "arbitrary"),
        name="swiglu_ffn",
    )(*args)


def _ab_in_kernel(x_ref, mod_ref, g_ref, w_ref, u_ref, z_ref, xbc_ref, dt_ref, *, widths):
    h = _norm_mod(x_ref[...], g_ref[...], mod_ref[3:4, :], mod_ref[4:5, :]).astype(BF16)
    p = _dot(h, w_ref[...])
    c0 = 0
    for ref, wd in zip((u_ref, z_ref, xbc_ref, dt_ref), widths):
        ref[...] = p[:, c0:c0 + wd]
        c0 += wd


def _ab_in_proj(x, modl, g, w_in, pool_dim, d_ssm, conv_dim, n_heads):
    bsz, seq, d = x.shape
    main = pool_dim + d_ssm + conv_dim
    w_dt = jnp.pad(w_in[:, main:main + n_heads], ((0, 0), (0, LANES - n_heads)))
    w = jnp.concatenate([w_in[:, :main], w_dt], axis=1).astype(BF16)
    widths = (pool_dim, d_ssm, conv_dim, LANES)
    return pl.pallas_call(
        functools.partial(_ab_in_kernel, widths=widths),
        grid=(bsz, seq // TM_PROJ),
        in_specs=[
            pl.BlockSpec((None, TM_PROJ, d), lambda b, l: (b, l, 0)),
            pl.BlockSpec((None, N_MOD, d), lambda b, l: (b, 0, 0)),
            _const_spec((1, d)),
            _const_spec(w.shape),
        ],
        out_specs=[pl.BlockSpec((None, TM_PROJ, wd), lambda b, l: (b, l, 0)) for wd in widths],
        out_shape=[jax.ShapeDtypeStruct((bsz, seq, wd), F32) for wd in widths],
        compiler_params=_params("arbitrary", "arbitrary"),
        name="pool_ssd_in_proj",
    )(x, modl, g.reshape(1, d), w)


def _softplus(x):
    return jnp.maximum(x, 0.0) + jnp.log1p(jnp.exp(-jnp.abs(x)))


def _mixer_kernel(u_ref, z_ref, xbc_ref, dtr_ref, pw_ref, ps_ref, cw_ref, cb_ref, dtb_ref,
                  aneg_ref, dsk_ref, ng_ref, rexp_ref, o_ref, xext, uext, st_ref, xc_ref,
                  *, pool_dim, d_ssm):
    tm = u_ref.shape[0]
    q = SSD_CHUNK
    n_groups = SSD_GROUPS
    gdim = d_ssm // n_groups
    heads_per_group = gdim // SSD_HEADDIM
    pool_gdim = pool_dim // len(POOL_WINDOWS)
    max_win = max(POOL_WINDOWS)
    lt = pl.program_id(1)

    @pl.when(lt == 0)
    def _():
        xext[0:SUBLANES, :] = jnp.zeros((SUBLANES, xext.shape[1]), F32)
        uext[0:max_win, :] = jnp.zeros((max_win, uext.shape[1]), F32)
        st_ref[...] = jnp.zeros_like(st_ref)

    xext[SUBLANES:SUBLANES + tm, :] = xbc_ref[...]
    ext = xext[...]
    acc = cb_ref[...] + cw_ref[CONV_K - 1:CONV_K, :] * xbc_ref[...]
    for k in range(CONV_K - 1):
        back = CONV_K - 1 - k
        acc = acc + cw_ref[k:k + 1, :] * pltpu.roll(ext, back, 0)[SUBLANES:, :]
    xc_ref[...] = _silu(acc)
    xext[0:SUBLANES, :] = xext[tm:tm + SUBLANES, :]

    uext[max_win:max_win + tm, :] = u_ref[...]
    tpos = (lt * tm + lax.broadcasted_iota(jnp.int32, (tm, pool_gdim), 0)).astype(F32)
    for gi, win in enumerate(POOL_WINDOWS):
        cols = slice(gi * pool_gdim, (gi + 1) * pool_gdim)
        cur = u_ref[:, cols]
        ws = uext[:, cols]
        span = 1
        while span < win:
            ws = ws + pltpu.roll(ws, span, 0)
            span *= 2
        ws = ws[max_win:, :]
        count = jnp.minimum(tpos + 1.0, float(win))
        dlt = ws / count - cur
        yp = _dot(dlt.astype(BF16), pw_ref[gi]) * ps_ref[:, cols]
        o_ref[:, cols] = yp.astype(BF16)
    uext[0:max_win, :] = uext[tm:tm + max_win, :]

    row = lax.broadcasted_iota(jnp.int32, (q, q), 0)
    col = lax.broadcasted_iota(jnp.int32, (q, q), 1)
    causal = row >= col
    tril = jnp.where(causal, 1.0, 0.0).astype(BF16)
    lane = lax.broadcasted_iota(jnp.int32, (1, LANES), 1)
    lo_half = lane < SSD_HEADDIM
    rexp = rexp_ref[...]

    def expand(v):
        hi, lo = _split_bf16(v, 2)
        return _dot(hi, rexp) + _dot(lo, rexp)

    for ci in range(tm // q):
        r0 = ci * q
        xs = xc_ref[r0:r0 + q, 0:d_ssm]
        dt = _softplus(dtr_ref[r0:r0 + q, :] + dtb_ref[...])
        adt = dt * aneg_ref[...]
        acs = sum(_dot(tril, p) for p in _split_bf16(adt, 3))
        acs_t = acs.T
        a_last = acs[q - 1:q, :]
        dt_e = expand(dt)
        eacs_e = expand(jnp.exp(acs))
        wend_e = expand(jnp.exp(a_last - acs) * dt)
        xdt = xs * dt_e
        xw = xs * wend_e
        for g in range(n_groups):
            gc = slice(g * gdim, (g + 1) * gdim)
            b0 = d_ssm + g * D_STATE
            c0 = d_ssm + n_groups * D_STATE + g * D_STATE
            bm = xc_ref[r0:r0 + q, b0:b0 + D_STATE]
            cm = xc_ref[r0:r0 + q, c0:c0 + D_STATE].astype(BF16)
            cb = _dot_nt(cm, bm.astype(BF16))
            s_prev = st_ref[g]
            y_off = _dot(cm, s_prev.astype(BF16)) * eacs_e[:, gc]
            ys = []
            for pr in range(heads_per_group // 2):
                h0 = g * heads_per_group + 2 * pr
                gs = []
                for hh in range(2):
                    h = h0 + hh
                    diff = acs[:, h:h + 1] - acs_t[h:h + 1, :]
                    dec = jnp.exp(jnp.where(causal, diff, -jnp.inf))
                    gs.append((cb * dec).astype(BF16))
                xp = xdt[:, h0 * SSD_HEADDIM:(h0 + 2) * SSD_HEADDIM]
                xbd = jnp.concatenate([jnp.where(lo_half, xp, 0.0), jnp.where(lo_half, 0.0, xp)],
                                      axis=0).astype(BF16)
                ys.append(_dot(jnp.concatenate(gs, axis=1), xbd))
            y = jnp.concatenate(ys, axis=1) + y_off
            st_ref[g] = eacs_e[q - 1:q, gc] * s_prev + _dot(bm.T.astype(BF16), xw[:, gc].astype(BF16))
            y = y + dsk_ref[:, gc] * xs[:, gc]
            y = y * _silu(z_ref[r0:r0 + q, gc])
            ms = jnp.mean(y * y, axis=-1, keepdims=True)
            y = y * lax.rsqrt(ms + RMS_EPS) * ng_ref[:, gc]
            o_ref[r0:r0 + q, pool_dim + g * gdim:pool_dim + (g + 1) * gdim] = y.astype(BF16)


def _mixer_core(u, z, xbc, dtr, pool_w, pool_scale, conv_w, conv_b, dt_bias, a_log, d_skip, norm_g):
    bsz, seq, pool_dim = u.shape
    d_ssm = z.shape[-1]
    conv_dim = xbc.shape[-1]
    n_heads = dt_bias.shape[0]
    pad = LANES - n_heads
    dtb = jnp.pad(dt_bias.astype(F32), (0, pad)).reshape(1, LANES)
    aneg = jnp.pad(-jnp.exp(a_log.astype(F32)), (0, pad)).reshape(1, LANES)
    dsk = jnp.repeat(d_skip.astype(F32), SSD_HEADDIM).reshape(1, d_ssm)
    rexp = (jnp.arange(LANES)[:, None] == (jnp.arange(d_ssm) // SSD_HEADDIM)[None, :]).astype(BF16)
    consts = [pool_w.astype(BF16), pool_scale.reshape(1, pool_dim), conv_w, conv_b.reshape(1, conv_dim),
              dtb, aneg, dsk, norm_g.reshape(1, d_ssm), rexp]
    widths = (pool_dim, d_ssm, conv_dim, LANES)
    max_win = max(POOL_WINDOWS)
    return pl.pallas_call(
        functools.partial(_mixer_kernel, pool_dim=pool_dim, d_ssm=d_ssm),
        grid=(bsz, seq // TM_MIX),
        in_specs=[pl.BlockSpec((None, TM_MIX, wd), lambda b, l: (b, l, 0)) for wd in widths]
        + [_const_spec(a.shape) for a in consts],
        out_specs=pl.BlockSpec((None, TM_MIX, pool_dim + d_ssm), lambda b, l: (b, l, 0)),
        out_shape=jax.ShapeDtypeStruct((bsz, seq, pool_dim + d_ssm), BF16),
        scratch_shapes=[
            pltpu.VMEM((SUBLANES + TM_MIX, conv_dim), F32),
            pltpu.VMEM((max_win + TM_MIX, pool_dim), F32),
            pltpu.VMEM((SSD_GROUPS, D_STATE, d_ssm // SSD_GROUPS), F32),
            pltpu.VMEM((TM_MIX, conv_dim), F32),
        ],
        compiler_params=_params("arbitrary", "arbitrary"),
        name="pool_ssd_core",
    )(u, z, xbc, dtr, *consts)


def _mla_proj_kernel(x_ref, mod_ref, g_ref, pos_ref, win_ref, qg_ref, wuq_ref, kvg_ref, wk_ref,
                     wv_ref, invf_ref, q_ref, k_ref, v_ref, *, q_lora, kv_lora, scale):
    h = _norm_mod(x_ref[...], g_ref[...], mod_ref[3:4, :], mod_ref[4:5, :]).astype(BF16)
    p = _dot(h, win_ref[...])
    q_a = p[:, 0:q_lora]
    kv_a = p[:, q_lora:q_lora + kv_lora]
    k_r = p[:, q_lora + kv_lora:q_lora + kv_lora + HEAD_PAD]

    def rms(t, gain):
        return t * lax.rsqrt(jnp.mean(t * t, axis=-1, keepdims=True) + RMS_EPS) * gain

    qn = rms(q_a, qg_ref[...]).astype(BF16)
    kvn = rms(kv_a, kvg_ref[...]).astype(BF16)

    ang = pos_ref[...] * invf_ref[...]
    cs = jnp.cos(ang)
    sn = jnp.sin(ang)
    lane = lax.broadcasted_iota(jnp.int32, (1, HEAD_PAD), 1)
    first = (lane >= ROPE_LANE0) & (lane < ROPE_LANE0 + HALF_ROPE)
    second = (lane >= ROPE_LANE0 + HALF_ROPE) & (lane < ROPE_LANE0 + QK_ROPE)
    s_first = jnp.where(first, -sn, 0.0)
    s_second = jnp.where(second, sn, 0.0)

    def rope(t, mult):
        up = pltpu.roll(t, HEAD_PAD - HALF_ROPE, 1)
        down = pltpu.roll(t, HALF_ROPE, 1)
        return (t * cs + up * s_first + down * s_second) * mult

    qf = _dot(qn, wuq_ref[...])
    kf = _dot(kvn, wk_ref[...])
    k_rot = rope(k_r, 1.0)
    n_heads = q_ref.shape[0]
    for hd in range(n_heads):
        cols = slice(hd * HEAD_PAD, (hd + 1) * HEAD_PAD)
        q_ref[hd] = rope(qf[:, cols], scale).astype(BF16)
        k_ref[hd] = (kf[:, cols] + k_rot).astype(BF16)
    vt = _dot_nt(wv_ref[...], kvn)
    for hp in range(v_ref.shape[0]):
        v_ref[hp] = vt[hp * LANES:(hp + 1) * LANES, :].astype(BF16)


def _mla_proj(x, modl, g, positions, w_in, q_norm_g, w_uq, kv_norm_g, w_ukv):
    bsz, seq, d = x.shape
    q_lora = w_uq.shape[0]
    kv_lora = w_ukv.shape[0]
    nh = MLA_HEADS
    w_rope = jnp.pad(w_in[:, q_lora + kv_lora:], ((0, 0), (ROPE_LANE0, HEAD_PAD - ROPE_LANE0 - QK_ROPE)))
    win = jnp.concatenate([w_in[:, :q_lora + kv_lora], w_rope], axis=1).astype(BF16)
    wuq = jnp.pad(w_uq.reshape(q_lora, nh, QK_NOPE + QK_ROPE),
                  ((0, 0), (0, 0), (0, HEAD_PAD - QK_NOPE - QK_ROPE))).reshape(q_lora, nh * HEAD_PAD)
    wkv = w_ukv.reshape(kv_lora, nh, QK_NOPE + V_DIM)
    wk = jnp.pad(wkv[:, :, :QK_NOPE], ((0, 0), (0, 0), (0, HEAD_PAD - QK_NOPE))).reshape(kv_lora, nh * HEAD_PAD)
    wv = wkv[:, :, QK_NOPE:].reshape(kv_lora, nh * V_DIM).T
    inv_freq = ROPE_THETA ** (-jnp.arange(0, QK_ROPE, 2, dtype=F32) / QK_ROPE)
    invf = jnp.zeros((HEAD_PAD,), F32).at[ROPE_LANE0:ROPE_LANE0 + QK_ROPE].set(jnp.tile(inv_freq, 2))
    pos = positions.astype(F32).reshape(bsz, seq, 1)
    scale = math.log2(math.e) / math.sqrt(QK_NOPE + QK_ROPE)
    consts = [win, q_norm_g.reshape(1, q_lora), wuq.astype(BF16), kv_norm_g.reshape(1, kv_lora),
              wk.astype(BF16), wv.astype(BF16), invf.reshape(1, HEAD_PAD)]
    n_vp = nh * V_DIM // LANES
    return pl.pallas_call(
        functools.partial(_mla_proj_kernel, q_lora=q_lora, kv_lora=kv_lora, scale=scale),
        grid=(bsz, seq // TM_PROJ),
        in_specs=[
            pl.BlockSpec((None, TM_PROJ, d), lambda b, l: (b, l, 0)),
            pl.BlockSpec((None, N_MOD, d), lambda b, l: (b, 0, 0)),
            _const_spec((1, d)),
            pl.BlockSpec((None, TM_PROJ, 1), lambda b, l: (b, l, 0)),
        ] + [_const_spec(a.shape) for a in consts],
        out_specs=[
            pl.BlockSpec((None, nh, TM_PROJ, HEAD_PAD), lambda b, l: (b, 0, l, 0)),
            pl.BlockSpec((None, nh, TM_PROJ, HEAD_PAD), lambda b, l: (b, 0, l, 0)),
            pl.BlockSpec((None, n_vp, LANES, TM_PROJ), lambda b, l: (b, 0, 0, l)),
        ],
        out_shape=[
            jax.ShapeDtypeStruct((bsz, nh, seq, HEAD_PAD), BF16),
            jax.ShapeDtypeStruct((bsz, nh, seq, HEAD_PAD), BF16),
            jax.ShapeDtypeStruct((bsz, n_vp, LANES, seq), BF16),
        ],
        compiler_params=_params("arbitrary", "arbitrary"),
        name="mla_qkv_proj",
    )(x, modl, g.reshape(1, d), pos, *consts)


def _attn_kernel(q_ref, k_ref, vt_ref, o_ref, sa_ref, sb_ref, pa_ref, pb_ref):
    seq = vt_ref.shape[2]
    n_heads = q_ref.shape[0]
    assert TQ == 2 * TK
    n_groups = TQ // LANES
    diag_groups = TK // LANES
    row_minus_lane = (lax.broadcasted_iota(jnp.int32, (TK, LANES), 0)
                      - lax.broadcasted_iota(jnp.int32, (TK, LANES), 1))
    ones_rows = jnp.ones((BF16_SUBLANES, TK), BF16)

    def load_q(q0):
        return [q_ref[hh, pl.ds(q0, TQ), :] for hh in range(n_heads)]

    def scores(dst_ref, qs, k0):
        for hh in range(n_heads):
            dst_ref[hh] = _dot_nt(k_ref[hh, pl.ds(k0, TK), :], qs[hh])

    def softmax(state, s_ref, p_ref, groups):
        out = []
        for hh in range(n_heads):
            m_old = state[3 * hh]
            ms, alphas = [], []
            for grp, delta in groups:
                cols = slice(grp * LANES, (grp + 1) * LANES)
                st = s_ref[hh, :, cols]
                if delta is not None:
                    st = jnp.where(row_minus_lane <= delta, st, -jnp.inf)
                m_prev = m_old[:, cols]
                m_new = jnp.maximum(m_prev, jnp.max(st, axis=0, keepdims=True))
                ms.append(m_new)
                alphas.append(jnp.exp2(m_prev - m_new))
                p_ref[hh, :, cols] = jnp.exp2(st - m_new).astype(BF16)
            out.append((jnp.concatenate(ms, axis=1), jnp.concatenate(alphas, axis=1)))
        return out

    def accumulate(state, soft, p_ref, k0, lo):
        vts = [vt_ref[hp, :, pl.ds(k0, TK)] for hp in range(n_heads // 2)]
        new = []
        for hh in range(n_heads):
            m_old, l_old, acc = state[3 * hh:3 * hh + 3]
            m_new, alpha = soft[hh]
            v_rows = vts[hh // 2][(hh % 2) * V_DIM:(hh % 2 + 1) * V_DIM, :]
            lhs = jnp.concatenate([v_rows, ones_rows], axis=0)
            pv = _dot(lhs, p_ref[hh, :, lo:])
            upd = [m_new, alpha * l_old[:, lo:] + pv[V_DIM:V_DIM + 1, :], acc[:, lo:] * alpha + pv[:V_DIM, :]]
            if lo:
                upd = [jnp.concatenate([old[:, :lo], u], axis=1) for old, u in zip((m_old, l_old, acc), upd)]
            new += upd
        return tuple(new)

    below = [(grp, None) for grp in range(n_groups)]
    diag0 = [(grp, grp * LANES if grp < diag_groups else None) for grp in range(n_groups)]
    diag1 = [(grp, (grp - diag_groups) * LANES) for grp in range(diag_groups, n_groups)]

    def q_tile(qi, carry):
        q0 = pl.multiple_of(qi * TQ, TQ)
        qs = load_q(q0)

        def pair(t, state):
            k0 = pl.multiple_of(2 * t * TK, 2 * TK)
            soft = softmax(state, sa_ref, pa_ref, below)
            scores(sa_ref, qs, k0 + 2 * TK)
            state = accumulate(state, soft, pa_ref, k0, 0)
            soft = softmax(state, sb_ref, pb_ref, below)
            scores(sb_ref, qs, k0 + 3 * TK)
            return accumulate(state, soft, pb_ref, k0 + TK, 0)

        init = (jnp.full((1, TQ), -jnp.inf, F32), jnp.zeros((1, TQ), F32),
                jnp.zeros((V_DIM, TQ), F32)) * n_heads
        state = lax.fori_loop(0, qi, pair, init)
        next_qs = load_q(pl.multiple_of(jnp.minimum(q0 + TQ, seq - TQ), TQ))
        soft = softmax(state, sa_ref, pa_ref, diag0)
        scores(sa_ref, next_qs, 0)
        state = accumulate(state, soft, pa_ref, q0, 0)
        soft = softmax(state, sb_ref, pb_ref, diag1)
        scores(sb_ref, next_qs, TK)
        state = accumulate(state, soft, pb_ref, q0 + TK, diag_groups * LANES)
        out_t = jnp.concatenate([state[3 * hh + 2] / state[3 * hh + 1] for hh in range(n_heads)], axis=0)
        o_ref[pl.ds(q0, TQ), :] = out_t.T.astype(BF16)
        return carry

    first_qs = load_q(0)
    scores(sa_ref, first_qs, 0)
    scores(sb_ref, first_qs, TK)
    lax.fori_loop(0, seq // TQ, q_tile, 0)


def _attention(q, k, v):
    bsz, nh, seq, _ = q.shape
    return pl.pallas_call(
        _attn_kernel,
        grid=(bsz, nh // ATTN_HEADS),
        in_specs=[
            pl.BlockSpec((None, ATTN_HEADS, seq, HEAD_PAD), lambda b, hg: (b, hg, 0, 0)),
            pl.BlockSpec((None, ATTN_HEADS, seq, HEAD_PAD), lambda b, hg: (b, hg, 0, 0)),
            pl.BlockSpec((None, ATTN_HEADS // 2, LANES, seq), lambda b, hg: (b, hg, 0, 0)),
        ],
        out_specs=pl.BlockSpec((None, seq, ATTN_HEADS * V_DIM), lambda b, hg: (b, 0, hg)),
        out_shape=jax.ShapeDtypeStruct((bsz, seq, nh * V_DIM), BF16),
        scratch_shapes=([pltpu.VMEM((ATTN_HEADS, TK, TQ), F32)] * 2
                        + [pltpu.VMEM((ATTN_HEADS, TK, TQ), BF16)] * 2),
        compiler_params=_params("arbitrary", "arbitrary"),
        name="mla_causal_attention",
    )(q, k, v)


def kernel(x, c, positions, mod_w, mod_b, norm_g, ffn_w13, ffn_w2, ab_w_in, pool_w, pool_scale, ssd_conv_w, ssd_conv_b, ssd_dt_bias, ssd_a_log, ssd_d, ssd_norm_g, ab_w_out, mla_w_in, mla_q_norm_g, mla_w_uq, mla_kv_norm_g, mla_w_ukv, mla_w_o, final_norm_g):
    depth = mod_w.shape[0]
    mod = _modulation(c, mod_w, mod_b)
    for i in range(depth):
        modl = mod[i]
        j = i // 2
        x = _ffn(x, modl, norm_g[i, 0], ffn_w13[i, 0], ffn_w2[i, 0], sub=0)
        if i % 2 == 0:
            pool_dim = pool_scale.shape[-1]
            d_ssm = ssd_norm_g.shape[-1]
            conv_dim = ssd_conv_b.shape[-1]
            n_heads = ssd_dt_bias.shape[-1]
            u, z, xbc, dtr = _ab_in_proj(x, modl, norm_g[i, 1], ab_w_in[j], pool_dim, d_ssm, conv_dim, n_heads)
            y = _mixer_core(u, z, xbc, dtr, pool_w[j], pool_scale[j], ssd_conv_w[j], ssd_conv_b[j],
                            ssd_dt_bias[j], ssd_a_log[j], ssd_d[j], ssd_norm_g[j])
            mixer = (y, ab_w_out[j])
        else:
            q, k, v = _mla_proj(x, modl, norm_g[i, 1], positions, mla_w_in[j], mla_q_norm_g[j],
                                mla_w_uq[j], mla_kv_norm_g[j], mla_w_ukv[j])
            mixer = (_attention(q, k, v), mla_w_o[j])
        last = i == depth - 1
        x = _ffn(x, modl, norm_g[i, 2], ffn_w13[i, 1], ffn_w2[i, 1], sub=2, mixer=mixer,
                 final_g=final_norm_g if last else None)
    return x
```

```python
import functools
import math

import jax
import jax.numpy as jnp
from jax import lax
from jax.experimental import pallas as pl
from jax.experimental.pallas import tpu as pltpu

F32 = jnp.float32
BF16 = jnp.bfloat16

RMS_EPS = 1e-6
FFN_RES = 0.5
POOL_WINDOWS = (2, 4, 8, 16)
SSD_HEADDIM = 64
SSD_GROUPS = 2
D_STATE = 128
CONV_K = 4
SSD_CHUNK = 128
MLA_HEADS = 16
QK_NOPE = 64
QK_ROPE = 32
V_DIM = 64
ROPE_THETA = 10000.0
N_MOD = 9

LANES = 128
SUBLANES = 8
BF16_SUBLANES = 16
VMEM_LIMIT_BYTES = 56 * 1024 * 1024

TM_FFN = 512
FFN_CHUNK = 256
TM_PROJ = 512
TM_MIX = 256
TM_MOD = 1024
TQ = 1024
TK = 256
ATTN_HEADS = 4
HALF_ROPE = QK_ROPE // 2
HEAD_PAD = 128
ROPE_LANE0 = QK_NOPE


def _params(*sem, flags=None):
    return pltpu.CompilerParams(dimension_semantics=sem, vmem_limit_bytes=VMEM_LIMIT_BYTES, flags=flags)


def _const_spec(shape):
    nd = len(shape)
    return pl.BlockSpec(shape, lambda *_: (0,) * nd, pipeline_mode=pl.Buffered(1))


def _silu(a):
    return a * jax.nn.sigmoid(a)


def _dot(a, b):
    return jnp.dot(a, b, preferred_element_type=F32)


def _dot_nt(a, b):
    return lax.dot_general(a, b, (((1,), (1,)), ((), ())), preferred_element_type=F32)


def _norm_mod(x, g, shift, scale):
    ms = jnp.mean(x * x, axis=-1, keepdims=True)
    y = x * lax.rsqrt(ms + RMS_EPS) * g
    return y * (1.0 + scale) + shift


def _split_bf16(x, parts):
    out = []
    r = x
    for _ in range(parts):
        p = r.astype(BF16)
        out.append(p)
        r = r - p.astype(F32)
    return out


def _mod_kernel(c_ref, w_ref, b_ref, o_ref):
    ca = _split_bf16(_silu(c_ref[...]), 2)
    w = _split_bf16(w_ref[...], 2)
    acc = b_ref[...]
    for i, j in ((1, 0), (0, 1), (0, 0)):
        acc = acc + _dot(ca[i], w[j])
    o_ref[...] = acc


def _modulation(c, mod_w, mod_b):
    depth, d, n = mod_w.shape
    bsz = c.shape[0]
    out = pl.pallas_call(
        _mod_kernel,
        grid=(depth, n // TM_MOD),
        in_specs=[
            pl.BlockSpec((bsz, d), lambda i, j: (0, 0)),
            pl.BlockSpec((None, d, TM_MOD), lambda i, j: (i, 0, j)),
            pl.BlockSpec((None, 1, TM_MOD), lambda i, j: (i, 0, j)),
        ],
        out_specs=pl.BlockSpec((None, bsz, TM_MOD), lambda i, j: (i, 0, j)),
        out_shape=jax.ShapeDtypeStruct((depth, bsz, n), F32),
        compiler_params=_params("arbitrary", "arbitrary"),
        name="adaln_modulation",
    )(c, mod_w, mod_b.reshape(depth, 1, n))
    return out.reshape(depth, bsz, N_MOD, d)


def _ffn_kernel(*refs, sub, d_ff, final, mixed):
    x_ref, mod_ref, g_ref, w13_ref, w2_ref = refs[:5]
    o_ref, h_ref, acc_ref = refs[-3:]
    extra = refs[5:-3]
    x = x_ref[...]
    if mixed:
        y_ref, wo_ref = extra[:2]
        x = x + mod_ref[5:6, :] * _dot(y_ref[...], wo_ref[...])
    shift = mod_ref[3 * sub:3 * sub + 1, :]
    scale = mod_ref[3 * sub + 1:3 * sub + 2, :]
    gate = mod_ref[3 * sub + 2:3 * sub + 3, :]
    h_ref[...] = _norm_mod(x, g_ref[...], shift, scale).astype(BF16)
    for c0 in range(0, d_ff, FFN_CHUNK):
        h = h_ref[...]
        a = _dot(h, w13_ref[:, c0:c0 + FFN_CHUNK])
        b = _dot(h, w13_ref[:, d_ff + c0:d_ff + c0 + FFN_CHUNK])
        part = _dot((_silu(a) * b).astype(BF16), w2_ref[c0:c0 + FFN_CHUNK, :])
        if c0 == 0:
            acc_ref[...] = part
        else:
            acc_ref[...] += part
    y = x + (FFN_RES * gate) * acc_ref[...]
    if final:
        fg_ref = extra[-1]
        ms = jnp.mean(y * y, axis=-1, keepdims=True)
        y = y * lax.rsqrt(ms + RMS_EPS) * fg_ref[...]
    o_ref[...] = y


def _ffn(x, modl, g, w13, w2, sub, mixer=None, final_g=None):
    bsz, seq, d = x.shape
    d_ff = w2.shape[0]
    assert d_ff % FFN_CHUNK == 0
    in_specs = [
        pl.BlockSpec((None, TM_FFN, d), lambda b, l: (b, l, 0)),
        pl.BlockSpec((None, N_MOD, d), lambda b, l: (b, 0, 0)),
        _const_spec((1, d)),
        _const_spec((d, 2 * d_ff)),
        _const_spec((d_ff, d)),
    ]
    args = [x, modl, g.reshape(1, d), w13.astype(BF16), w2.astype(BF16)]
    if mixer is not None:
        y, w_out = mixer
        k = y.shape[-1]
        in_specs += [pl.BlockSpec((None, TM_FFN, k), lambda b, l: (b, l, 0)), _const_spec((k, d))]
        args += [y, w_out.astype(BF16)]
    if final_g is not None:
        in_specs.append(_const_spec((1, d)))
        args.append(final_g.reshape(1, d))
    return pl.pallas_call(
        functools.partial(_ffn_kernel, sub=sub, d_ff=d_ff, final=final_g is not None,
                          mixed=mixer is not None),
        grid=(bsz, seq // TM_FFN),
        in_specs=in_specs,
        out_specs=pl.BlockSpec((None, TM_FFN, d), lambda b, l: (b, l, 0)),
        out_shape=jax.ShapeDtypeStruct((bsz, seq, d), F32),
        scratch_shapes=[pltpu.VMEM((TM_FFN, d), BF16), pltpu.VMEM((TM_FFN, d), F32)],
        compiler_params=_params("arbitrary", "arbitrary"),
        name="swiglu_ffn",
    )(*args)


def _ab_in_kernel(x_ref, mod_ref, g_ref, w_ref, u_ref, z_ref, xbc_ref, dt_ref, *, widths):
    h = _norm_mod(x_ref[...], g_ref[...], mod_ref[3:4, :], mod_ref[4:5, :]).astype(BF16)
    p = _dot(h, w_ref[...])
    c0 = 0
    for ref, wd in zip((u_ref, z_ref, xbc_ref, dt_ref), widths):
        ref[...] = p[:, c0:c0 + wd]
        c0 += wd


def _ab_in_proj(x, modl, g, w_in, pool_dim, d_ssm, conv_dim, n_heads):
    bsz, seq, d = x.shape
    main = pool_dim + d_ssm + conv_dim
    w_dt = jnp.pad(w_in[:, main:main + n_heads], ((0, 0), (0, LANES - n_heads)))
    w = jnp.concatenate([w_in[:, :main], w_dt], axis=1).astype(BF16)
    widths = (pool_dim, d_ssm, conv_dim, LANES)
    return pl.pallas_call(
        functools.partial(_ab_in_kernel, widths=widths),
        grid=(bsz, seq // TM_PROJ),
        in_specs=[
            pl.BlockSpec((None, TM_PROJ, d), lambda b, l: (b, l, 0)),
            pl.BlockSpec((None, N_MOD, d), lambda b, l: (b, 0, 0)),
            _const_spec((1, d)),
            _const_spec(w.shape),
        ],
        out_specs=[pl.BlockSpec((None, TM_PROJ, wd), lambda b, l: (b, l, 0)) for wd in widths],
        out_shape=[jax.ShapeDtypeStruct((bsz, seq, wd), F32) for wd in widths],
        compiler_params=_params("arbitrary", "arbitrary"),
        name="pool_ssd_in_proj",
    )(x, modl, g.reshape(1, d), w)


def _softplus(x):
    return jnp.maximum(x, 0.0) + jnp.log1p(jnp.exp(-jnp.abs(x)))


def _mixer_kernel(u_ref, z_ref, xbc_ref, dtr_ref, pw_ref, ps_ref, cw_ref, cb_ref, dtb_ref,
                  aneg_ref, dsk_ref, ng_ref, rexp_ref, o_ref, xext, uext, st_ref, xc_ref,
                  *, pool_dim, d_ssm):
    tm = u_ref.shape[0]
    q = SSD_CHUNK
    n_groups = SSD_GROUPS
    gdim = d_ssm // n_groups
    heads_per_group = gdim // SSD_HEADDIM
    pool_gdim = pool_dim // len(POOL_WINDOWS)
    max_win = max(POOL_WINDOWS)
    lt = pl.program_id(1)

    @pl.when(lt == 0)
    def _():
        xext[0:SUBLANES, :] = jnp.zeros((SUBLANES, xext.shape[1]), F32)
        uext[0:max_win, :] = jnp.zeros((max_win, uext.shape[1]), F32)
        st_ref[...] = jnp.zeros_like(st_ref)

    xext[SUBLANES:SUBLANES + tm, :] = xbc_ref[...]
    ext = xext[...]
    acc = cb_ref[...] + cw_ref[CONV_K - 1:CONV_K, :] * xbc_ref[...]
    for k in range(CONV_K - 1):
        back = CONV_K - 1 - k
        acc = acc + cw_ref[k:k + 1, :] * pltpu.roll(ext, back, 0)[SUBLANES:, :]
    xc_ref[...] = _silu(acc)
    xext[0:SUBLANES, :] = xext[tm:tm + SUBLANES, :]

    uext[max_win:max_win + tm, :] = u_ref[...]
    tpos = (lt * tm + lax.broadcasted_iota(jnp.int32, (tm, pool_gdim), 0)).astype(F32)
    for gi, win in enumerate(POOL_WINDOWS):
        cols = slice(gi * pool_gdim, (gi + 1) * pool_gdim)
        cur = u_ref[:, cols]
        assert win & (win - 1) == 0 and win <= max_win
        ws = uext[:, cols]
        span = 1
        while span < win:
            ws = ws + pltpu.roll(ws, span, 0)
            span *= 2
        ws = ws[max_win:, :]
        count = jnp.minimum(tpos + 1.0, float(win))
        dlt = ws / count - cur
        yp = _dot(dlt.astype(BF16), pw_ref[gi]) * ps_ref[:, cols]
        o_ref[:, cols] = yp.astype(BF16)
    uext[0:max_win, :] = uext[tm:tm + max_win, :]

    row = lax.broadcasted_iota(jnp.int32, (q, q), 0)
    col = lax.broadcasted_iota(jnp.int32, (q, q), 1)
    causal = row >= col
    tril = jnp.where(causal, 1.0, 0.0).astype(BF16)
    lane = lax.broadcasted_iota(jnp.int32, (1, LANES), 1)
    lo_half = lane < SSD_HEADDIM
    rexp = rexp_ref[...]

    def expand(v):
        hi, lo = _split_bf16(v, 2)
        return _dot(hi, rexp) + _dot(lo, rexp)

    for ci in range(tm // q):
        r0 = ci * q
        xs = xc_ref[r0:r0 + q, 0:d_ssm]
        dt = _softplus(dtr_ref[r0:r0 + q, :] + dtb_ref[...])
        adt = dt * aneg_ref[...]
        acs = sum(_dot(tril, p) for p in _split_bf16(adt, 3))
        acs_t = acs.T
        a_last = acs[q - 1:q, :]
        dt_e = expand(dt)
        eacs_e = expand(jnp.exp(acs))
        wend_e = expand(jnp.exp(a_last - acs) * dt)
        xdt = xs * dt_e
        xw = xs * wend_e
        for g in range(n_groups):
            gc = slice(g * gdim, (g + 1) * gdim)
            b0 = d_ssm + g * D_STATE
            c0 = d_ssm + n_groups * D_STATE + g * D_STATE
            bm = xc_ref[r0:r0 + q, b0:b0 + D_STATE]
            cm = xc_ref[r0:r0 + q, c0:c0 + D_STATE].astype(BF16)
            cb = _dot_nt(cm, bm.astype(BF16))
            s_prev = st_ref[g]
            y_off = _dot(cm, s_prev.astype(BF16)) * eacs_e[:, gc]
            ys = []
            for pr in range(heads_per_group // 2):
                h0 = g * heads_per_group + 2 * pr
                gs = []
                for hh in range(2):
                    h = h0 + hh
                    diff = acs[:, h:h + 1] - acs_t[h:h + 1, :]
                    dec = jnp.exp(jnp.where(causal, diff, -jnp.inf))
                    gs.append((cb * dec).astype(BF16))
                xp = xdt[:, h0 * SSD_HEADDIM:(h0 + 2) * SSD_HEADDIM]
                xbd = jnp.concatenate([jnp.where(lo_half, xp, 0.0), jnp.where(lo_half, 0.0, xp)],
                                      axis=0).astype(BF16)
                ys.append(_dot(jnp.concatenate(gs, axis=1), xbd))
            y = jnp.concatenate(ys, axis=1) + y_off
            st_ref[g] = eacs_e[q - 1:q, gc] * s_prev + _dot(bm.T.astype(BF16), xw[:, gc].astype(BF16))
            y = y + dsk_ref[:, gc] * xs[:, gc]
            y = y * _silu(z_ref[r0:r0 + q, gc])
            ms = jnp.mean(y * y, axis=-1, keepdims=True)
            y = y * lax.rsqrt(ms + RMS_EPS) * ng_ref[:, gc]
            o_ref[r0:r0 + q, pool_dim + g * gdim:pool_dim + (g + 1) * gdim] = y.astype(BF16)


def _mixer_core(u, z, xbc, dtr, pool_w, pool_scale, conv_w, conv_b, dt_bias, a_log, d_skip, norm_g):
    bsz, seq, pool_dim = u.shape
    d_ssm = z.shape[-1]
    conv_dim = xbc.shape[-1]
    n_heads = dt_bias.shape[0]
    pad = LANES - n_heads
    dtb = jnp.pad(dt_bias.astype(F32), (0, pad)).reshape(1, LANES)
    aneg = jnp.pad(-jnp.exp(a_log.astype(F32)), (0, pad)).reshape(1, LANES)
    dsk = jnp.repeat(d_skip.astype(F32), SSD_HEADDIM).reshape(1, d_ssm)
    rexp = (jnp.arange(LANES)[:, None] == (jnp.arange(d_ssm) // SSD_HEADDIM)[None, :]).astype(BF16)
    consts = [pool_w.astype(BF16), pool_scale.reshape(1, pool_dim), conv_w, conv_b.reshape(1, conv_dim),
              dtb, aneg, dsk, norm_g.reshape(1, d_ssm), rexp]
    widths = (pool_dim, d_ssm, conv_dim, LANES)
    max_win = max(POOL_WINDOWS)
    return pl.pallas_call(
        functools.partial(_mixer_kernel, pool_dim=pool_dim, d_ssm=d_ssm),
        grid=(bsz, seq // TM_MIX),
        in_specs=[pl.BlockSpec((None, TM_MIX, wd), lambda b, l: (b, l, 0)) for wd in widths]
        + [_const_spec(a.shape) for a in consts],
        out_specs=pl.BlockSpec((None, TM_MIX, pool_dim + d_ssm), lambda b, l: (b, l, 0)),
        out_shape=jax.ShapeDtypeStruct((bsz, seq, pool_dim + d_ssm), BF16),
        scratch_shapes=[
            pltpu.VMEM((SUBLANES + TM_MIX, conv_dim), F32),
            pltpu.VMEM((max_win + TM_MIX, pool_dim), F32),
            pltpu.VMEM((SSD_GROUPS, D_STATE, d_ssm // SSD_GROUPS), F32),
            pltpu.VMEM((TM_MIX, conv_dim), F32),
        ],
        compiler_params=_params("arbitrary", "arbitrary"),
        name="pool_ssd_core",
    )(u, z, xbc, dtr, *consts)


def _mla_proj_kernel(x_ref, mod_ref, g_ref, pos_ref, win_ref, qg_ref, wuq_ref, kvg_ref, wk_ref,
                     wv_ref, invf_ref, q_ref, k_ref, v_ref, *, q_lora, kv_lora, scale):
    h = _norm_mod(x_ref[...], g_ref[...], mod_ref[3:4, :], mod_ref[4:5, :]).astype(BF16)
    p = _dot(h, win_ref[...])
    q_a = p[:, 0:q_lora]
    kv_a = p[:, q_lora:q_lora + kv_lora]
    k_r = p[:, q_lora + kv_lora:q_lora + kv_lora + HEAD_PAD]

    def rms(t, gain):
        return t * lax.rsqrt(jnp.mean(t * t, axis=-1, keepdims=True) + RMS_EPS) * gain

    qn = rms(q_a, qg_ref[...]).astype(BF16)
    kvn = rms(kv_a, kvg_ref[...]).astype(BF16)

    ang = pos_ref[...] * invf_ref[...]
    cs = jnp.cos(ang)
    sn = jnp.sin(ang)
    lane = lax.broadcasted_iota(jnp.int32, (1, HEAD_PAD), 1)
    first = (lane >= ROPE_LANE0) & (lane < ROPE_LANE0 + HALF_ROPE)
    second = (lane >= ROPE_LANE0 + HALF_ROPE) & (lane < ROPE_LANE0 + QK_ROPE)
    s_first = jnp.where(first, -sn, 0.0)
    s_second = jnp.where(second, sn, 0.0)

    def rope(t, mult):
        up = pltpu.roll(t, HEAD_PAD - HALF_ROPE, 1)
        down = pltpu.roll(t, HALF_ROPE, 1)
        return (t * cs + up * s_first + down * s_second) * mult

    qf = _dot(qn, wuq_ref[...])
    kf = _dot(kvn, wk_ref[...])
    k_rot = rope(k_r, 1.0)
    n_heads = q_ref.shape[0]
    for hd in range(n_heads):
        cols = slice(hd * HEAD_PAD, (hd + 1) * HEAD_PAD)
        q_ref[hd] = rope(qf[:, cols], scale).astype(BF16)
        k_ref[hd] = (kf[:, cols] + k_rot).astype(BF16)
    vt = _dot_nt(wv_ref[...], kvn)
    for hp in range(v_ref.shape[0]):
        v_ref[hp] = vt[hp * LANES:(hp + 1) * LANES, :].astype(BF16)


def _mla_proj(x, modl, g, positions, w_in, q_norm_g, w_uq, kv_norm_g, w_ukv):
    bsz, seq, d = x.shape
    q_lora = w_uq.shape[0]
    kv_lora = w_ukv.shape[0]
    nh = MLA_HEADS
    w_rope = jnp.pad(w_in[:, q_lora + kv_lora:], ((0, 0), (ROPE_LANE0, HEAD_PAD - ROPE_LANE0 - QK_ROPE)))
    win = jnp.concatenate([w_in[:, :q_lora + kv_lora], w_rope], axis=1).astype(BF16)
    wuq = jnp.pad(w_uq.reshape(q_lora, nh, QK_NOPE + QK_ROPE),
                  ((0, 0), (0, 0), (0, HEAD_PAD - QK_NOPE - QK_ROPE))).reshape(q_lora, nh * HEAD_PAD)
    wkv = w_ukv.reshape(kv_lora, nh, QK_NOPE + V_DIM)
    wk = jnp.pad(wkv[:, :, :QK_NOPE], ((0, 0), (0, 0), (0, HEAD_PAD - QK_NOPE))).reshape(kv_lora, nh * HEAD_PAD)
    wv = wkv[:, :, QK_NOPE:].reshape(kv_lora, nh * V_DIM).T
    inv_freq = ROPE_THETA ** (-jnp.arange(0, QK_ROPE, 2, dtype=F32) / QK_ROPE)
    invf = jnp.zeros((HEAD_PAD,), F32).at[ROPE_LANE0:ROPE_LANE0 + QK_ROPE].set(jnp.tile(inv_freq, 2))
    pos = positions.astype(F32).reshape(bsz, seq, 1)
    scale = math.log2(math.e) / math.sqrt(QK_NOPE + QK_ROPE)
    consts = [win, q_norm_g.reshape(1, q_lora), wuq.astype(BF16), kv_norm_g.reshape(1, kv_lora),
              wk.astype(BF16), wv.astype(BF16), invf.reshape(1, HEAD_PAD)]
    n_vp = nh * V_DIM // LANES
    return pl.pallas_call(
        functools.partial(_mla_proj_kernel, q_lora=q_lora, kv_lora=kv_lora, scale=scale),
        grid=(bsz, seq // TM_PROJ),
        in_specs=[
            pl.BlockSpec((None, TM_PROJ, d), lambda b, l: (b, l, 0)),
            pl.BlockSpec((None, N_MOD, d), lambda b, l: (b, 0, 0)),
            _const_spec((1, d)),
            pl.BlockSpec((None, TM_PROJ, 1), lambda b, l: (b, l, 0)),
        ] + [_const_spec(a.shape) for a in consts],
        out_specs=[
            pl.BlockSpec((None, nh, TM_PROJ, HEAD_PAD), lambda b, l: (b, 0, l, 0)),
            pl.BlockSpec((None, nh, TM_PROJ, HEAD_PAD), lambda b, l: (b, 0, l, 0)),
            pl.BlockSpec((None, n_vp, LANES, TM_PROJ), lambda b, l: (b, 0, 0, l)),
        ],
        out_shape=[
            jax.ShapeDtypeStruct((bsz, nh, seq, HEAD_PAD), BF16),
            jax.ShapeDtypeStruct((bsz, nh, seq, HEAD_PAD), BF16),
            jax.ShapeDtypeStruct((bsz, n_vp, LANES, seq), BF16),
        ],
        compiler_params=_params("arbitrary", "arbitrary"),
        name="mla_qkv_proj",
    )(x, modl, g.reshape(1, d), pos, *consts)


def _attn_kernel(q_ref, k_ref, vt_ref, o_ref, sa_ref, sb_ref, pa_ref, pb_ref):
    seq = vt_ref.shape[2]
    n_heads = q_ref.shape[0]
    blocks = TQ // TK
    assert TQ % (2 * TK) == 0
    n_groups = TQ // LANES
    block_groups = TK // LANES
    row_minus_lane = (lax.broadcasted_iota(jnp.int32, (TK, LANES), 0)
                      - lax.broadcasted_iota(jnp.int32, (TK, LANES), 1))
    ones_rows = jnp.ones((BF16_SUBLANES, TK), BF16)

    def load_q(q0):
        return [q_ref[hh, pl.ds(q0, TQ), :] for hh in range(n_heads)]

    def scores(dst_ref, qs, k0, lo=0):
        for hh in range(n_heads):
            dst_ref[hh, :, lo:] = _dot_nt(k_ref[hh, pl.ds(k0, TK), :], qs[hh][lo:])

    def softmax(state, s_ref, p_ref, groups):
        out = []
        for hh in range(n_heads):
            m_old = state[3 * hh]
            ms, alphas = [], []
            for grp, delta in groups:
                cols = slice(grp * LANES, (grp + 1) * LANES)
                st = s_ref[hh, :, cols]
                if delta is not None:
                    st = jnp.where(row_minus_lane <= delta, st, -jnp.inf)
                m_prev = m_old[:, cols]
                m_new = jnp.maximum(m_prev, jnp.max(st, axis=0, keepdims=True))
                ms.append(m_new)
                alphas.append(jnp.exp2(m_prev - m_new))
                p_ref[hh, :, cols] = jnp.exp2(st - m_new).astype(BF16)
            out.append((jnp.concatenate(ms, axis=1), jnp.concatenate(alphas, axis=1)))
        return out

    def accumulate(state, soft, p_ref, k0, lo):
        vts = [vt_ref[hp, :, pl.ds(k0, TK)] for hp in range(n_heads // 2)]
        new = []
        for hh in range(n_heads):
            m_old, l_old, acc = state[3 * hh:3 * hh + 3]
            m_new, alpha = soft[hh]
            v_rows = vts[hh // 2][(hh % 2) * V_DIM:(hh % 2 + 1) * V_DIM, :]
            lhs = jnp.concatenate([v_rows, ones_rows], axis=0)
            pv = _dot(lhs, p_ref[hh, :, lo:])
            upd = [m_new, alpha * l_old[:, lo:] + pv[V_DIM:V_DIM + 1, :], acc[:, lo:] * alpha + pv[:V_DIM, :]]
            if lo:
                upd = [jnp.concatenate([old[:, :lo], u], axis=1) for old, u in zip((m_old, l_old, acc), upd)]
            new += upd
        return tuple(new)

    below = [(grp, None) for grp in range(n_groups)]

    def diagonal(d):
        first = d * block_groups
        return [(grp, (grp - first) * LANES if grp < first + block_groups else None)
                for grp in range(first, n_groups)]

    def q_tile(qi, carry):
        q0 = pl.multiple_of(qi * TQ, TQ)
        qs = load_q(q0)

        def pair(t, state):
            k0 = pl.multiple_of(2 * t * TK, 2 * TK)
            soft = softmax(state, sa_ref, pa_ref, below)
            scores(sa_ref, qs, k0 + 2 * TK)
            state = accumulate(state, soft, pa_ref, k0, 0)
            soft = softmax(state, sb_ref, pb_ref, below)
            scores(sb_ref, qs, k0 + 3 * TK)
            return accumulate(state, soft, pb_ref, k0 + TK, 0)

        init = (jnp.full((1, TQ), -jnp.inf, F32), jnp.zeros((1, TQ), F32),
                jnp.zeros((V_DIM, TQ), F32)) * n_heads
        state = lax.fori_loop(0, qi * (blocks // 2), pair, init)
        next_qs = load_q(pl.multiple_of(jnp.minimum(q0 + TQ, seq - TQ), TQ))
        for d in range(blocks):
            s_ref, p_ref = (sa_ref, pa_ref) if d % 2 == 0 else (sb_ref, pb_ref)
            soft = softmax(state, s_ref, p_ref, diagonal(d))
            if d + 2 < blocks:
                scores(s_ref, qs, q0 + (d + 2) * TK, (d + 2) * TK)
            else:
                scores(s_ref, next_qs, (d + 2 - blocks) * TK)
            state = accumulate(state, soft, p_ref, q0 + d * TK, d * TK)
        out_t = jnp.concatenate([state[3 * hh + 2] / state[3 * hh + 1] for hh in range(n_heads)], axis=0)
        o_ref[pl.ds(q0, TQ), :] = out_t.T.astype(BF16)
        return carry

    first_qs = load_q(0)
    scores(sa_ref, first_qs, 0)
    scores(sb_ref, first_qs, TK)
    lax.fori_loop(0, seq // TQ, q_tile, 0)


def _attention(q, k, v):
    bsz, nh, seq, _ = q.shape
    return pl.pallas_call(
        _attn_kernel,
        grid=(bsz, nh // ATTN_HEADS),
        in_specs=[
            pl.BlockSpec((None, ATTN_HEADS, seq, HEAD_PAD), lambda b, hg: (b, hg, 0, 0)),
            pl.BlockSpec((None, ATTN_HEADS, seq, HEAD_PAD), lambda b, hg: (b, hg, 0, 0)),
            pl.BlockSpec((None, ATTN_HEADS // 2, LANES, seq), lambda b, hg: (b, hg, 0, 0)),
        ],
        out_specs=pl.BlockSpec((None, seq, ATTN_HEADS * V_DIM), lambda b, hg: (b, 0, hg)),
        out_shape=jax.ShapeDtypeStruct((bsz, seq, nh * V_DIM), BF16),
        scratch_shapes=([pltpu.VMEM((ATTN_HEADS, TK, TQ), F32)] * 2
                        + [pltpu.VMEM((ATTN_HEADS, TK, TQ), BF16)] * 2),
        compiler_params=_params("arbitrary", "arbitrary"),
        name="mla_causal_attention",
    )(q, k, v)


def kernel(x, c, positions, mod_w, mod_b, norm_g, ffn_w13, ffn_w2, ab_w_in, pool_w, pool_scale, ssd_conv_w, ssd_conv_b, ssd_dt_bias, ssd_a_log, ssd_d, ssd_norm_g, ab_w_out, mla_w_in, mla_q_norm_g, mla_w_uq, mla_kv_norm_g, mla_w_ukv, mla_w_o, final_norm_g):
    depth = mod_w.shape[0]
    mod = _modulation(c, mod_w, mod_b)
    for i in range(depth):
        modl = mod[i]
        j = i // 2
        x = _ffn(x, modl, norm_g[i, 0], ffn_w13[i, 0], ffn_w2[i, 0], sub=0)
        if i % 2 == 0:
            pool_dim = pool_scale.shape[-1]
            d_ssm = ssd_norm_g.shape[-1]
            conv_dim = ssd_conv_b.shape[-1]
            n_heads = ssd_dt_bias.shape[-1]
            u, z, xbc, dtr = _ab_in_proj(x, modl, norm_g[i, 1], ab_w_in[j], pool_dim, d_ssm, conv_dim, n_heads)
            y = _mixer_core(u, z, xbc, dtr, pool_w[j], pool_scale[j], ssd_conv_w[j], ssd_conv_b[j],
                            ssd_dt_bias[j], ssd_a_log[j], ssd_d[j], ssd_norm_g[j])
            mixer = (y, ab_w_out[j])
        else:
            q, k, v = _mla_proj(x, modl, norm_g[i, 1], positions, mla_w_in[j], mla_q_norm_g[j],
                                mla_w_uq[j], mla_kv_norm_g[j], mla_w_ukv[j])
            mixer = (_attention(q, k, v), mla_w_o[j])
        last = i == depth - 1
        x = _ffn(x, modl, norm_g[i, 2], ffn_w13[i, 1], ffn_w2[i, 1], sub=2, mixer=mixer,
                 final_g=final_norm_g if last else None)
    return x
```

```python
import functools
import math

import jax
import jax.numpy as jnp
from jax import lax
from jax.experimental import pallas as pl
from jax.experimental.pallas import tpu as pltpu

F32 = jnp.float32
BF16 = jnp.bfloat16

RMS_EPS = 1e-6
FFN_RES = 0.5
POOL_WINDOWS = (2, 4, 8, 16)
SSD_HEADDIM = 64
SSD_GROUPS = 2
D_STATE = 128
CONV_K = 4
SSD_CHUNK = 128
MLA_HEADS = 16
QK_NOPE = 64
QK_ROPE = 32
V_DIM = 64
ROPE_THETA = 10000.0
N_MOD = 9

LANES = 128
SUBLANES = 8
BF16_SUBLANES = 16
VMEM_LIMIT_BYTES = 56 * 1024 * 1024

TM_FFN = 1024
FFN_CHUNK = 256
FFN_SUBTILES = 2
TM_PROJ = 512
TM_MIX = 512
TM_MOD = 1024
TQ = 1024
TK = 256
ATTN_HEADS = 4
HALF_ROPE = QK_ROPE // 2
HEAD_PAD = 128
ROPE_LANE0 = QK_NOPE


def _params(*sem, flags=None):
    return pltpu.CompilerParams(dimension_semantics=sem, vmem_limit_bytes=VMEM_LIMIT_BYTES, flags=flags)


def _const_spec(shape):
    nd = len(shape)
    return pl.BlockSpec(shape, lambda *_: (0,) * nd, pipeline_mode=pl.Buffered(1))


def _silu(a):
    half = 0.5 * a
    return half + half * jnp.tanh(half)


def _dot(a, b):
    return jnp.dot(a, b, preferred_element_type=F32)


def _dot_nt(a, b):
    return lax.dot_general(a, b, (((1,), (1,)), ((), ())), preferred_element_type=F32)


def _norm_mod(x, g, shift, scale):
    ms = jnp.mean(x * x, axis=-1, keepdims=True)
    y = x * lax.rsqrt(ms + RMS_EPS) * g
    return y * (1.0 + scale) + shift


def _split_bf16(x, parts):
    out = []
    r = x
    for _ in range(parts):
        p = r.astype(BF16)
        out.append(p)
        r = r - p.astype(F32)
    return out


def _mod_kernel(c_ref, w_ref, b_ref, o_ref):
    ca = _split_bf16(_silu(c_ref[...]), 2)
    w = _split_bf16(w_ref[...], 2)
    acc = b_ref[...]
    for i, j in ((1, 0), (0, 1), (0, 0)):
        acc = acc + _dot(ca[i], w[j])
    o_ref[...] = acc


def _modulation(c, mod_w, mod_b):
    depth, d, n = mod_w.shape
    bsz = c.shape[0]
    out = pl.pallas_call(
        _mod_kernel,
        grid=(depth, n // TM_MOD),
        in_specs=[
            pl.BlockSpec((bsz, d), lambda i, j: (0, 0)),
            pl.BlockSpec((None, d, TM_MOD), lambda i, j: (i, 0, j)),
            pl.BlockSpec((None, 1, TM_MOD), lambda i, j: (i, 0, j)),
        ],
        out_specs=pl.BlockSpec((None, bsz, TM_MOD), lambda i, j: (i, 0, j)),
        out_shape=jax.ShapeDtypeStruct((depth, bsz, n), F32),
        compiler_params=_params("arbitrary", "arbitrary"),
        name="adaln_modulation",
    )(c, mod_w, mod_b.reshape(depth, 1, n))
    return out.reshape(depth, bsz, N_MOD, d)


def _ffn_kernel(*refs, sub, d_ff, final, mixed):
    x_ref, mod_ref, g_ref, w13_ref, w2_ref = refs[:5]
    o_ref, h_ref, acc_ref = refs[-3:]
    extra = refs[5:-3]
    shift = mod_ref[3 * sub:3 * sub + 1, :]
    scale = mod_ref[3 * sub + 1:3 * sub + 2, :]
    gate = mod_ref[3 * sub + 2:3 * sub + 3, :]
    rows = x_ref.shape[0] // FFN_SUBTILES

    def residual_in(i):
        x = x_ref[i * rows:(i + 1) * rows, :]
        if mixed:
            y_ref, wo_ref = extra[:2]
            x = x + mod_ref[5:6, :] * _dot(y_ref[i * rows:(i + 1) * rows, :], wo_ref[...])
        return x

    def chunk(i, c0):
        h = h_ref[i]
        a = _dot(h, w13_ref[:, c0:c0 + FFN_CHUNK])
        b = _dot(h, w13_ref[:, d_ff + c0:d_ff + c0 + FFN_CHUNK])
        part = _dot((_silu(a) * b).astype(BF16), w2_ref[c0:c0 + FFN_CHUNK, :])
        if c0 == 0:
            acc_ref[i] = part
        else:
            acc_ref[i] += part

    def finish(i, x):
        y = x + (FFN_RES * gate) * acc_ref[i]
        if final:
            fg_ref = extra[-1]
            ms = jnp.mean(y * y, axis=-1, keepdims=True)
            y = y * lax.rsqrt(ms + RMS_EPS) * fg_ref[...]
        o_ref[i * rows:(i + 1) * rows, :] = y

    starts = list(range(0, d_ff, FFN_CHUNK))
    xs = []
    for i in range(FFN_SUBTILES):
        xs.append(residual_in(i))
        h_ref[i] = _norm_mod(xs[i], g_ref[...], shift, scale).astype(BF16)
        chunk(i, starts[0])
    for c0 in starts[1:-1]:
        for i in range(FFN_SUBTILES):
            chunk(i, c0)
    for i in range(FFN_SUBTILES):
        chunk(i, starts[-1])
        finish(i, xs[i])


def _ffn(x, modl, g, w13, w2, layer, sub, mixer=None, final_g=None):
    bsz, seq, d = x.shape
    d_ff = w2.shape[-2]
    assert d_ff % FFN_CHUNK == 0
    which = sub // 2

    def weight_spec(rows, cols):
        return pl.BlockSpec((None, None, rows, cols), lambda b, l: (layer, which, 0, 0),
                            pipeline_mode=pl.Buffered(1))

    in_specs = [
        pl.BlockSpec((None, TM_FFN, d), lambda b, l: (b, l, 0)),
        pl.BlockSpec((None, N_MOD, d), lambda b, l: (b, 0, 0)),
        _const_spec((1, d)),
        weight_spec(d, 2 * d_ff),
        weight_spec(d_ff, d),
    ]
    args = [x, modl, g.reshape(1, d), w13, w2]
    if mixer is not None:
        y, w_out = mixer
        k = y.shape[-1]
        in_specs += [pl.BlockSpec((None, TM_FFN, k), lambda b, l: (b, l, 0)), _const_spec((k, d))]
        args += [y, w_out.astype(BF16)]
    if final_g is not None:
        in_specs.append(_const_spec((1, d)))
        args.append(final_g.reshape(1, d))
    return pl.pallas_call(
        functools.partial(_ffn_kernel, sub=sub, d_ff=d_ff, final=final_g is not None,
                          mixed=mixer is not None),
        grid=(bsz, seq // TM_FFN),
        in_specs=in_specs,
        out_specs=pl.BlockSpec((None, TM_FFN, d), lambda b, l: (b, l, 0)),
        out_shape=jax.ShapeDtypeStruct((bsz, seq, d), F32),
        scratch_shapes=[pltpu.VMEM((FFN_SUBTILES, TM_FFN // FFN_SUBTILES, d), BF16),
                        pltpu.VMEM((FFN_SUBTILES, TM_FFN // FFN_SUBTILES, d), F32)],
        compiler_params=_params("arbitrary", "arbitrary"),
        name="swiglu_ffn",
    )(*args)


def _ab_in_kernel(x_ref, mod_ref, g_ref, w_ref, u_ref, z_ref, xbc_ref, dt_ref, *, widths):
    h = _norm_mod(x_ref[...], g_ref[...], mod_ref[3:4, :], mod_ref[4:5, :]).astype(BF16)
    p = _dot(h, w_ref[...])
    c0 = 0
    for ref, wd in zip((u_ref, z_ref, xbc_ref, dt_ref), widths):
        ref[...] = p[:, c0:c0 + wd]
        c0 += wd


def _ab_in_proj(x, modl, g, w_in, pool_dim, d_ssm, conv_dim, n_heads):
    bsz, seq, d = x.shape
    main = pool_dim + d_ssm + conv_dim
    w_dt = jnp.pad(w_in[:, main:main + n_heads], ((0, 0), (0, LANES - n_heads)))
    w = jnp.concatenate([w_in[:, :main], w_dt], axis=1).astype(BF16)
    widths = (pool_dim, d_ssm, conv_dim, LANES)
    return pl.pallas_call(
        functools.partial(_ab_in_kernel, widths=widths),
        grid=(bsz, seq // TM_PROJ),
        in_specs=[
            pl.BlockSpec((None, TM_PROJ, d), lambda b, l: (b, l, 0)),
            pl.BlockSpec((None, N_MOD, d), lambda b, l: (b, 0, 0)),
            _const_spec((1, d)),
            _const_spec(w.shape),
        ],
        out_specs=[pl.BlockSpec((None, TM_PROJ, wd), lambda b, l: (b, l, 0)) for wd in widths],
        out_shape=[jax.ShapeDtypeStruct((bsz, seq, wd), F32) for wd in widths],
        compiler_params=_params("arbitrary", "arbitrary"),
        name="pool_ssd_in_proj",
    )(x, modl, g.reshape(1, d), w)


def _softplus(x):
    return jnp.maximum(x, 0.0) + jnp.log1p(jnp.exp(-jnp.abs(x)))


def _mixer_kernel(u_ref, z_ref, xbc_ref, dtr_ref, pw_ref, ps_ref, cw_ref, cb_ref, dtb_ref,
                  aneg_ref, dsk_ref, ng_ref, rexp_ref, o_ref, xext, uext, st_ref, xc_ref,
                  *, pool_dim, d_ssm):
    tm = u_ref.shape[0]
    q = SSD_CHUNK
    n_groups = SSD_GROUPS
    gdim = d_ssm // n_groups
    heads_per_group = gdim // SSD_HEADDIM
    pool_gdim = pool_dim // len(POOL_WINDOWS)
    max_win = max(POOL_WINDOWS)
    lt = pl.program_id(1)

    @pl.when(lt == 0)
    def _():
        xext[0:SUBLANES, :] = jnp.zeros((SUBLANES, xext.shape[1]), F32)
        uext[0:max_win, :] = jnp.zeros((max_win, uext.shape[1]), F32)
        st_ref[...] = jnp.zeros_like(st_ref)

    xext[SUBLANES:SUBLANES + tm, :] = xbc_ref[...]
    ext = xext[...]
    acc = cb_ref[...] + cw_ref[CONV_K - 1:CONV_K, :] * xbc_ref[...]
    for k in range(CONV_K - 1):
        back = CONV_K - 1 - k
        acc = acc + cw_ref[k:k + 1, :] * pltpu.roll(ext, back, 0)[SUBLANES:, :]
    xc_ref[...] = _silu(acc)
    xext[0:SUBLANES, :] = xext[tm:tm + SUBLANES, :]

    uext[max_win:max_win + tm, :] = u_ref[...]
    tpos = (lt * tm + lax.broadcasted_iota(jnp.int32, (tm, pool_gdim), 0)).astype(F32)
    for gi, win in enumerate(POOL_WINDOWS):
        cols = slice(gi * pool_gdim, (gi + 1) * pool_gdim)
        cur = u_ref[:, cols]
        assert win & (win - 1) == 0 and win <= max_win
        ws = uext[:, cols]
        span = 1
        while span < win:
            ws = ws + pltpu.roll(ws, span, 0)
            span *= 2
        ws = ws[max_win:, :]
        count = jnp.minimum(tpos + 1.0, float(win))
        dlt = ws / count - cur
        yp = _dot(dlt.astype(BF16), pw_ref[gi]) * ps_ref[:, cols]
        o_ref[:, cols] = yp.astype(BF16)
    uext[0:max_win, :] = uext[tm:tm + max_win, :]

    row = lax.broadcasted_iota(jnp.int32, (q, q), 0)
    col = lax.broadcasted_iota(jnp.int32, (q, q), 1)
    causal = row >= col
    tril = jnp.where(causal, 1.0, 0.0).astype(BF16)
    lane = lax.broadcasted_iota(jnp.int32, (1, LANES), 1)
    lo_half = lane < SSD_HEADDIM
    rexp = rexp_ref[...]

    def expand(v):
        hi, lo = _split_bf16(v, 2)
        return _dot(hi, rexp) + _dot(lo, rexp)

    for ci in range(tm // q):
        r0 = ci * q
        xs = xc_ref[r0:r0 + q, 0:d_ssm]
        dt = _softplus(dtr_ref[r0:r0 + q, :] + dtb_ref[...])
        adt = dt * aneg_ref[...]
        acs = sum(_dot(tril, p) for p in _split_bf16(adt, 3))
        acs_t = acs.T
        a_last = acs[q - 1:q, :]
        dt_e = expand(dt)
        eacs_e = expand(jnp.exp(acs))
        wend_e = expand(jnp.exp(a_last - acs) * dt)
        xdt = xs * dt_e
        xw = xs * wend_e
        for g in range(n_groups):
            gc = slice(g * gdim, (g + 1) * gdim)
            b0 = d_ssm + g * D_STATE
            c0 = d_ssm + n_groups * D_STATE + g * D_STATE
            bm = xc_ref[r0:r0 + q, b0:b0 + D_STATE]
            cm = xc_ref[r0:r0 + q, c0:c0 + D_STATE].astype(BF16)
            cb = _dot_nt(cm, bm.astype(BF16))
            s_prev = st_ref[g]
            y_off = _dot(cm, s_prev.astype(BF16)) * eacs_e[:, gc]
            ys = []
            for pr in range(heads_per_group // 2):
                h0 = g * heads_per_group + 2 * pr
                gs = []
                for hh in range(2):
                    h = h0 + hh
                    diff = acs[:, h:h + 1] - acs_t[h:h + 1, :]
                    dec = jnp.exp(jnp.where(causal, diff, -jnp.inf))
                    gs.append((cb * dec).astype(BF16))
                xp = xdt[:, h0 * SSD_HEADDIM:(h0 + 2) * SSD_HEADDIM]
                xbd = jnp.concatenate([jnp.where(lo_half, xp, 0.0), jnp.where(lo_half, 0.0, xp)],
                                      axis=0).astype(BF16)
                ys.append(_dot(jnp.concatenate(gs, axis=1), xbd))
            y = jnp.concatenate(ys, axis=1) + y_off
            st_ref[g] = eacs_e[q - 1:q, gc] * s_prev + _dot(bm.T.astype(BF16), xw[:, gc].astype(BF16))
            y = y + dsk_ref[:, gc] * xs[:, gc]
            y = y * _silu(z_ref[r0:r0 + q, gc])
            ms = jnp.mean(y * y, axis=-1, keepdims=True)
            y = y * lax.rsqrt(ms + RMS_EPS) * ng_ref[:, gc]
            o_ref[r0:r0 + q, pool_dim + g * gdim:pool_dim + (g + 1) * gdim] = y.astype(BF16)


def _mixer_core(u, z, xbc, dtr, pool_w, pool_scale, conv_w, conv_b, dt_bias, a_log, d_skip, norm_g):
    bsz, seq, pool_dim = u.shape
    d_ssm = z.shape[-1]
    conv_dim = xbc.shape[-1]
    n_heads = dt_bias.shape[0]
    pad = LANES - n_heads
    dtb = jnp.pad(dt_bias.astype(F32), (0, pad)).reshape(1, LANES)
    aneg = jnp.pad(-jnp.exp(a_log.astype(F32)), (0, pad)).reshape(1, LANES)
    dsk = jnp.repeat(d_skip.astype(F32), SSD_HEADDIM).reshape(1, d_ssm)
    rexp = (jnp.arange(LANES)[:, None] == (jnp.arange(d_ssm) // SSD_HEADDIM)[None, :]).astype(BF16)
    consts = [pool_w.astype(BF16), pool_scale.reshape(1, pool_dim), conv_w, conv_b.reshape(1, conv_dim),
              dtb, aneg, dsk, norm_g.reshape(1, d_ssm), rexp]
    widths = (pool_dim, d_ssm, conv_dim, LANES)
    max_win = max(POOL_WINDOWS)
    return pl.pallas_call(
        functools.partial(_mixer_kernel, pool_dim=pool_dim, d_ssm=d_ssm),
        grid=(bsz, seq // TM_MIX),
        in_specs=[pl.BlockSpec((None, TM_MIX, wd), lambda b, l: (b, l, 0)) for wd in widths]
        + [_const_spec(a.shape) for a in consts],
        out_specs=pl.BlockSpec((None, TM_MIX, pool_dim + d_ssm), lambda b, l: (b, l, 0)),
        out_shape=jax.ShapeDtypeStruct((bsz, seq, pool_dim + d_ssm), BF16),
        scratch_shapes=[
            pltpu.VMEM((SUBLANES + TM_MIX, conv_dim), F32),
            pltpu.VMEM((max_win + TM_MIX, pool_dim), F32),
            pltpu.VMEM((SSD_GROUPS, D_STATE, d_ssm // SSD_GROUPS), F32),
            pltpu.VMEM((TM_MIX, conv_dim), F32),
        ],
        compiler_params=_params("arbitrary", "arbitrary"),
        name="pool_ssd_core",
    )(u, z, xbc, dtr, *consts)


def _mla_proj_kernel(x_ref, mod_ref, g_ref, pos_ref, win_ref, qg_ref, wuq_ref, kvg_ref, wk_ref,
                     wv_ref, invf_ref, q_ref, k_ref, v_ref, *, q_lora, kv_lora, scale):
    h = _norm_mod(x_ref[...], g_ref[...], mod_ref[3:4, :], mod_ref[4:5, :]).astype(BF16)
    p = _dot(h, win_ref[...])
    q_a = p[:, 0:q_lora]
    kv_a = p[:, q_lora:q_lora + kv_lora]
    k_r = p[:, q_lora + kv_lora:q_lora + kv_lora + HEAD_PAD]

    def rms(t, gain):
        return t * lax.rsqrt(jnp.mean(t * t, axis=-1, keepdims=True) + RMS_EPS) * gain

    qn = rms(q_a, qg_ref[...]).astype(BF16)
    kvn = rms(kv_a, kvg_ref[...]).astype(BF16)

    ang = pos_ref[...] * invf_ref[...]
    cs = jnp.cos(ang)
    sn = jnp.sin(ang)
    lane = lax.broadcasted_iota(jnp.int32, (1, HEAD_PAD), 1)
    first = (lane >= ROPE_LANE0) & (lane < ROPE_LANE0 + HALF_ROPE)
    second = (lane >= ROPE_LANE0 + HALF_ROPE) & (lane < ROPE_LANE0 + QK_ROPE)
    s_first = jnp.where(first, -sn, 0.0)
    s_second = jnp.where(second, sn, 0.0)

    def rope(t, mult):
        up = pltpu.roll(t, HEAD_PAD - HALF_ROPE, 1)
        down = pltpu.roll(t, HALF_ROPE, 1)
        return (t * cs + up * s_first + down * s_second) * mult

    qf = _dot(qn, wuq_ref[...])
    kf = _dot(kvn, wk_ref[...])
    k_rot = rope(k_r, 1.0)
    n_heads = q_ref.shape[0]
    for hd in range(n_heads):
        cols = slice(hd * HEAD_PAD, (hd + 1) * HEAD_PAD)
        q_ref[hd] = rope(qf[:, cols], scale).astype(BF16)
        k_ref[hd] = (kf[:, cols] + k_rot).astype(BF16)
    vt = _dot_nt(wv_ref[...], kvn)
    for hp in range(v_ref.shape[0]):
        v_ref[hp] = vt[hp * LANES:(hp + 1) * LANES, :].astype(BF16)


def _mla_proj(x, modl, g, positions, w_in, q_norm_g, w_uq, kv_norm_g, w_ukv):
    bsz, seq, d = x.shape
    q_lora = w_uq.shape[0]
    kv_lora = w_ukv.shape[0]
    nh = MLA_HEADS
    w_rope = jnp.pad(w_in[:, q_lora + kv_lora:], ((0, 0), (ROPE_LANE0, HEAD_PAD - ROPE_LANE0 - QK_ROPE)))
    win = jnp.concatenate([w_in[:, :q_lora + kv_lora], w_rope], axis=1).astype(BF16)
    wuq = jnp.pad(w_uq.reshape(q_lora, nh, QK_NOPE + QK_ROPE),
                  ((0, 0), (0, 0), (0, HEAD_PAD - QK_NOPE - QK_ROPE))).reshape(q_lora, nh * HEAD_PAD)
    wkv = w_ukv.reshape(kv_lora, nh, QK_NOPE + V_DIM)
    wk = jnp.pad(wkv[:, :, :QK_NOPE], ((0, 0), (0, 0), (0, HEAD_PAD - QK_NOPE))).reshape(kv_lora, nh * HEAD_PAD)
    wv = wkv[:, :, QK_NOPE:].reshape(kv_lora, nh * V_DIM).T
    inv_freq = ROPE_THETA ** (-jnp.arange(0, QK_ROPE, 2, dtype=F32) / QK_ROPE)
    invf = jnp.zeros((HEAD_PAD,), F32).at[ROPE_LANE0:ROPE_LANE0 + QK_ROPE].set(jnp.tile(inv_freq, 2))
    pos = positions.astype(F32).reshape(bsz, seq, 1)
    scale = math.log2(math.e) / math.sqrt(QK_NOPE + QK_ROPE)
    consts = [win, q_norm_g.reshape(1, q_lora), wuq.astype(BF16), kv_norm_g.reshape(1, kv_lora),
              wk.astype(BF16), wv.astype(BF16), invf.reshape(1, HEAD_PAD)]
    n_vp = nh * V_DIM // LANES
    return pl.pallas_call(
        functools.partial(_mla_proj_kernel, q_lora=q_lora, kv_lora=kv_lora, scale=scale),
        grid=(bsz, seq // TM_PROJ),
        in_specs=[
            pl.BlockSpec((None, TM_PROJ, d), lambda b, l: (b, l, 0)),
            pl.BlockSpec((None, N_MOD, d), lambda b, l: (b, 0, 0)),
            _const_spec((1, d)),
            pl.BlockSpec((None, TM_PROJ, 1), lambda b, l: (b, l, 0)),
        ] + [_const_spec(a.shape) for a in consts],
        out_specs=[
            pl.BlockSpec((None, nh, TM_PROJ, HEAD_PAD), lambda b, l: (b, 0, l, 0)),
            pl.BlockSpec((None, nh, TM_PROJ, HEAD_PAD), lambda b, l: (b, 0, l, 0)),
            pl.BlockSpec((None, n_vp, LANES, TM_PROJ), lambda b, l: (b, 0, 0, l)),
        ],
        out_shape=[
            jax.ShapeDtypeStruct((bsz, nh, seq, HEAD_PAD), BF16),
            jax.ShapeDtypeStruct((bsz, nh, seq, HEAD_PAD), BF16),
            jax.ShapeDtypeStruct((bsz, n_vp, LANES, seq), BF16),
        ],
        compiler_params=_params("arbitrary", "arbitrary"),
        name="mla_qkv_proj",
    )(x, modl, g.reshape(1, d), pos, *consts)


def _attn_kernel(q_ref, k_ref, vt_ref, o_ref, sa_ref, sb_ref, pa_ref, pb_ref):
    seq = vt_ref.shape[2]
    n_heads = q_ref.shape[0]
    blocks = TQ // TK
    assert TQ % (2 * TK) == 0
    n_groups = TQ // LANES
    block_groups = TK // LANES
    row_minus_lane = (lax.broadcasted_iota(jnp.int32, (TK, LANES), 0)
                      - lax.broadcasted_iota(jnp.int32, (TK, LANES), 1))
    ones_rows = jnp.ones((BF16_SUBLANES, TK), BF16)

    def load_q(q0):
        return [q_ref[hh, pl.ds(q0, TQ), :] for hh in range(n_heads)]

    def scores(dst_ref, qs, k0, lo=0):
        for hh in range(n_heads):
            dst_ref[hh, :, lo:] = _dot_nt(k_ref[hh, pl.ds(k0, TK), :], qs[hh][lo:])

    def softmax(state, s_ref, p_ref, groups):
        out = []
        for hh in range(n_heads):
            m_old = state[3 * hh]
            ms, alphas = [], []
            for grp, delta in groups:
                cols = slice(grp * LANES, (grp + 1) * LANES)
                st = s_ref[hh, :, cols]
                if delta is not None:
                    st = jnp.where(row_minus_lane <= delta, st, -jnp.inf)
                m_prev = m_old[:, cols]
                m_new = jnp.maximum(m_prev, jnp.max(st, axis=0, keepdims=True))
                ms.append(m_new)
                alphas.append(jnp.exp2(m_prev - m_new))
                p_ref[hh, :, cols] = jnp.exp2(st - m_new).astype(BF16)
            out.append((jnp.concatenate(ms, axis=1), jnp.concatenate(alphas, axis=1)))
        return out

    def accumulate(state, soft, p_ref, k0, lo):
        vts = [vt_ref[hp, :, pl.ds(k0, TK)] for hp in range(n_heads // 2)]
        new = []
        for hh in range(n_heads):
            m_old, l_old, acc = state[3 * hh:3 * hh + 3]
            m_new, alpha = soft[hh]
            v_rows = vts[hh // 2][(hh % 2) * V_DIM:(hh % 2 + 1) * V_DIM, :]
            lhs = jnp.concatenate([v_rows, ones_rows], axis=0)
            pv = _dot(lhs, p_ref[hh, :, lo:])
            upd = [m_new, alpha * l_old[:, lo:] + pv[V_DIM:V_DIM + 1, :], acc[:, lo:] * alpha + pv[:V_DIM, :]]
            if lo:
                upd = [jnp.concatenate([old[:, :lo], u], axis=1) for old, u in zip((m_old, l_old, acc), upd)]
            new += upd
        return tuple(new)

    below = [(grp, None) for grp in range(n_groups)]

    def diagonal(d):
        first = d * block_groups
        return [(grp, (grp - first) * LANES if grp < first + block_groups else None)
                for grp in range(first, n_groups)]

    def q_tile(qi, carry):
        q0 = pl.multiple_of(qi * TQ, TQ)
        qs = load_q(q0)

        def pair(t, state):
            k0 = pl.multiple_of(2 * t * TK, 2 * TK)
            soft = softmax(state, sa_ref, pa_ref, below)
            scores(sa_ref, qs, k0 + 2 * TK)
            state = accumulate(state, soft, pa_ref, k0, 0)
            soft = softmax(state, sb_ref, pb_ref, below)
            scores(sb_ref, qs, k0 + 3 * TK)
            return accumulate(state, soft, pb_ref, k0 + TK, 0)

        init = (jnp.full((1, TQ), -jnp.inf, F32), jnp.zeros((1, TQ), F32),
                jnp.zeros((V_DIM, TQ), F32)) * n_heads
        state = lax.fori_loop(0, qi * (blocks // 2), pair, init)
        next_qs = load_q(pl.multiple_of(jnp.minimum(q0 + TQ, seq - TQ), TQ))
        for d in range(blocks):
            s_ref, p_ref = (sa_ref, pa_ref) if d % 2 == 0 else (sb_ref, pb_ref)
            soft = softmax(state, s_ref, p_ref, diagonal(d))
            if d + 2 < blocks:
                scores(s_ref, qs, q0 + (d + 2) * TK, (d + 2) * TK)
            else:
                scores(s_ref, next_qs, (d + 2 - blocks) * TK)
            state = accumulate(state, soft, p_ref, q0 + d * TK, d * TK)
        out_t = jnp.concatenate([state[3 * hh + 2] / state[3 * hh + 1] for hh in range(n_heads)], axis=0)
        o_ref[pl.ds(q0, TQ), :] = out_t.T.astype(BF16)
        return carry

    first_qs = load_q(0)
    scores(sa_ref, first_qs, 0)
    scores(sb_ref, first_qs, TK)
    lax.fori_loop(0, seq // TQ, q_tile, 0)


def _attention(q, k, v):
    bsz, nh, seq, _ = q.shape
    return pl.pallas_call(
        _attn_kernel,
        grid=(bsz, nh // ATTN_HEADS),
        in_specs=[
            pl.BlockSpec((None, ATTN_HEADS, seq, HEAD_PAD), lambda b, hg: (b, hg, 0, 0)),
            pl.BlockSpec((None, ATTN_HEADS, seq, HEAD_PAD), lambda b, hg: (b, hg, 0, 0)),
            pl.BlockSpec((None, ATTN_HEADS // 2, LANES, seq), lambda b, hg: (b, hg, 0, 0)),
        ],
        out_specs=pl.BlockSpec((None, seq, ATTN_HEADS * V_DIM), lambda b, hg: (b, 0, hg)),
        out_shape=jax.ShapeDtypeStruct((bsz, seq, nh * V_DIM), BF16),
        scratch_shapes=([pltpu.VMEM((ATTN_HEADS, TK, TQ), F32)] * 2
                        + [pltpu.VMEM((ATTN_HEADS, TK, TQ), BF16)] * 2),
        compiler_params=_params("arbitrary", "arbitrary"),
        name="mla_causal_attention",
    )(q, k, v)


def kernel(x, c, positions, mod_w, mod_b, norm_g, ffn_w13, ffn_w2, ab_w_in, pool_w, pool_scale, ssd_conv_w, ssd_conv_b, ssd_dt_bias, ssd_a_log, ssd_d, ssd_norm_g, ab_w_out, mla_w_in, mla_q_norm_g, mla_w_uq, mla_kv_norm_g, mla_w_ukv, mla_w_o, final_norm_g):
    depth = mod_w.shape[0]
    mod = _modulation(c, mod_w, mod_b)
    w13, w2 = ffn_w13.astype(BF16), ffn_w2.astype(BF16)
    for i in range(depth):
        modl = mod[i]
        j = i // 2
        x = _ffn(x, modl, norm_g[i, 0], w13, w2, i, sub=0)
        if i % 2 == 0:
            pool_dim = pool_scale.shape[-1]
            d_ssm = ssd_norm_g.shape[-1]
            conv_dim = ssd_conv_b.shape[-1]
            n_heads = ssd_dt_bias.shape[-1]
            u, z, xbc, dtr = _ab_in_proj(x, modl, norm_g[i, 1], ab_w_in[j], pool_dim, d_ssm, conv_dim, n_heads)
            y = _mixer_core(u, z, xbc, dtr, pool_w[j], pool_scale[j], ssd_conv_w[j], ssd_conv_b[j],
                            ssd_dt_bias[j], ssd_a_log[j], ssd_d[j], ssd_norm_g[j])
            mixer = (y, ab_w_out[j])
        else:
            q, k, v = _mla_proj(x, modl, norm_g[i, 1], positions, mla_w_in[j], mla_q_norm_g[j],
                                mla_w_uq[j], mla_kv_norm_g[j], mla_w_ukv[j])
            mixer = (_attention(q, k, v), mla_w_o[j])
        last = i == depth - 1
        x = _ffn(x, modl, norm_g[i, 2], w13, w2, i, sub=2, mixer=mixer,
                 final_g=final_norm_g if last else None)
    return x
```

```python
import functools
import math

import jax
import jax.numpy as jnp
from jax import lax
from jax.experimental import pallas as pl
from jax.experimental.pallas import tpu as pltpu

F32 = jnp.float32
BF16 = jnp.bfloat16

RMS_EPS = 1e-6
FFN_RES = 0.5
POOL_WINDOWS = (2, 4, 8, 16)
SSD_HEADDIM = 64
SSD_GROUPS = 2
D_STATE = 128
CONV_K = 4
SSD_CHUNK = 128
MLA_HEADS = 16
QK_NOPE = 64
QK_ROPE = 32
V_DIM = 64
ROPE_THETA = 10000.0
N_MOD = 9

LANES = 128
SUBLANES = 8
BF16_SUBLANES = 16
VMEM_LIMIT_BYTES = 56 * 1024 * 1024

TM_FFN = 1024
FFN_CHUNK = 256
FFN_SUBTILES = 2
TM_PROJ = 512
PROJ_SUBTILES = 2
TM_MIX = 512
TM_MOD = 1024
TQ = 1024
TK = 256
ATTN_HEADS = 4
HALF_ROPE = QK_ROPE // 2
HEAD_PAD = 128
ROPE_LANE0 = QK_NOPE


def _params(*sem, flags=None):
    return pltpu.CompilerParams(dimension_semantics=sem, vmem_limit_bytes=VMEM_LIMIT_BYTES, flags=flags)


def _const_spec(shape):
    nd = len(shape)
    return pl.BlockSpec(shape, lambda *_: (0,) * nd, pipeline_mode=pl.Buffered(1))


def _silu(a):
    half = 0.5 * a
    return half + half * jnp.tanh(half)


def _dot(a, b):
    return jnp.dot(a, b, preferred_element_type=F32)


def _dot_nt(a, b):
    return lax.dot_general(a, b, (((1,), (1,)), ((), ())), preferred_element_type=F32)


def _norm_mod(x, g, shift, scale):
    ms = jnp.mean(x * x, axis=-1, keepdims=True)
    y = x * lax.rsqrt(ms + RMS_EPS) * g
    return y * (1.0 + scale) + shift


def _split_bf16(x, parts):
    out = []
    r = x
    for _ in range(parts):
        p = r.astype(BF16)
        out.append(p)
        r = r - p.astype(F32)
    return out


def _mod_kernel(c_ref, w_ref, b_ref, o_ref):
    ca = _split_bf16(_silu(c_ref[...]), 2)
    w = _split_bf16(w_ref[...], 2)
    acc = b_ref[...]
    for i, j in ((1, 0), (0, 1), (0, 0)):
        acc = acc + _dot(ca[i], w[j])
    o_ref[...] = acc


def _modulation(c, mod_w, mod_b):
    depth, d, n = mod_w.shape
    bsz = c.shape[0]
    out = pl.pallas_call(
        _mod_kernel,
        grid=(depth, n // TM_MOD),
        in_specs=[
            pl.BlockSpec((bsz, d), lambda i, j: (0, 0)),
            pl.BlockSpec((None, d, TM_MOD), lambda i, j: (i, 0, j)),
            pl.BlockSpec((None, 1, TM_MOD), lambda i, j: (i, 0, j)),
        ],
        out_specs=pl.BlockSpec((None, bsz, TM_MOD), lambda i, j: (i, 0, j)),
        out_shape=jax.ShapeDtypeStruct((depth, bsz, n), F32),
        compiler_params=_params("arbitrary", "arbitrary"),
        name="adaln_modulation",
    )(c, mod_w, mod_b.reshape(depth, 1, n))
    return out.reshape(depth, bsz, N_MOD, d)


def _ffn_kernel(*refs, sub, d_ff, final, mixed):
    x_ref, mod_ref, g_ref, w13_ref, w2_ref = refs[:5]
    o_ref, h_ref, acc_ref = refs[-3:]
    extra = refs[5:-3]
    shift = mod_ref[3 * sub:3 * sub + 1, :]
    scale = mod_ref[3 * sub + 1:3 * sub + 2, :]
    gate = mod_ref[3 * sub + 2:3 * sub + 3, :]
    rows = x_ref.shape[0] // FFN_SUBTILES

    def residual_in(i):
        x = x_ref[i * rows:(i + 1) * rows, :]
        if mixed:
            y_ref, wo_ref = extra[:2]
            x = x + mod_ref[5:6, :] * _dot(y_ref[i * rows:(i + 1) * rows, :], wo_ref[...])
        return x

    def chunk(i, c0):
        h = h_ref[i]
        a = _dot(h, w13_ref[:, c0:c0 + FFN_CHUNK])
        b = _dot(h, w13_ref[:, d_ff + c0:d_ff + c0 + FFN_CHUNK])
        part = _dot((_silu(a) * b).astype(BF16), w2_ref[c0:c0 + FFN_CHUNK, :])
        if c0 == 0:
            acc_ref[i] = part
        else:
            acc_ref[i] += part

    def finish(i, x):
        y = x + (FFN_RES * gate) * acc_ref[i]
        if final:
            fg_ref = extra[-1]
            ms = jnp.mean(y * y, axis=-1, keepdims=True)
            y = y * lax.rsqrt(ms + RMS_EPS) * fg_ref[...]
        o_ref[i * rows:(i + 1) * rows, :] = y

    starts = list(range(0, d_ff, FFN_CHUNK))
    xs = []
    for i in range(FFN_SUBTILES):
        xs.append(residual_in(i))
        h_ref[i] = _norm_mod(xs[i], g_ref[...], shift, scale).astype(BF16)
        chunk(i, starts[0])
    for c0 in starts[1:-1]:
        for i in range(FFN_SUBTILES):
            chunk(i, c0)
    for i in range(FFN_SUBTILES):
        chunk(i, starts[-1])
        finish(i, xs[i])


def _ffn(x, modl, g, w13, w2, layer, sub, mixer=None, final_g=None):
    bsz, seq, d = x.shape
    d_ff = w2.shape[-2]
    assert d_ff % FFN_CHUNK == 0
    which = sub // 2

    def weight_spec(rows, cols):
        return pl.BlockSpec((None, None, rows, cols), lambda b, l: (layer, which, 0, 0),
                            pipeline_mode=pl.Buffered(1))

    in_specs = [
        pl.BlockSpec((None, TM_FFN, d), lambda b, l: (b, l, 0)),
        pl.BlockSpec((None, N_MOD, d), lambda b, l: (b, 0, 0)),
        _const_spec((1, d)),
        weight_spec(d, 2 * d_ff),
        weight_spec(d_ff, d),
    ]
    args = [x, modl, g.reshape(1, d), w13, w2]
    if mixer is not None:
        y, w_out = mixer
        k = y.shape[-1]
        in_specs += [pl.BlockSpec((None, TM_FFN, k), lambda b, l: (b, l, 0)), _const_spec((k, d))]
        args += [y, w_out.astype(BF16)]
    if final_g is not None:
        in_specs.append(_const_spec((1, d)))
        args.append(final_g.reshape(1, d))
    return pl.pallas_call(
        functools.partial(_ffn_kernel, sub=sub, d_ff=d_ff, final=final_g is not None,
                          mixed=mixer is not None),
        grid=(bsz, seq // TM_FFN),
        in_specs=in_specs,
        out_specs=pl.BlockSpec((None, TM_FFN, d), lambda b, l: (b, l, 0)),
        out_shape=jax.ShapeDtypeStruct((bsz, seq, d), F32),
        scratch_shapes=[pltpu.VMEM((FFN_SUBTILES, TM_FFN // FFN_SUBTILES, d), BF16),
                        pltpu.VMEM((FFN_SUBTILES, TM_FFN // FFN_SUBTILES, d), F32)],
        compiler_params=_params("arbitrary", "arbitrary"),
        name="swiglu_ffn",
    )(*args)


def _ab_in_kernel(x_ref, mod_ref, g_ref, w_ref, u_ref, z_ref, xbc_ref, dt_ref, *, widths):
    rows = x_ref.shape[0] // PROJ_SUBTILES
    ps = []
    for i in range(PROJ_SUBTILES):
        r = slice(i * rows, (i + 1) * rows)
        h = _norm_mod(x_ref[r, :], g_ref[...], mod_ref[3:4, :], mod_ref[4:5, :]).astype(BF16)
        ps.append(_dot(h, w_ref[...]))
    for i, p in enumerate(ps):
        r = slice(i * rows, (i + 1) * rows)
        c0 = 0
        for ref, wd in zip((u_ref, z_ref, xbc_ref, dt_ref), widths):
            ref[r, :] = p[:, c0:c0 + wd]
            c0 += wd


def _ab_in_proj(x, modl, g, w_in, pool_dim, d_ssm, conv_dim, n_heads):
    bsz, seq, d = x.shape
    main = pool_dim + d_ssm + conv_dim
    w_dt = jnp.pad(w_in[:, main:main + n_heads], ((0, 0), (0, LANES - n_heads)))
    w = jnp.concatenate([w_in[:, :main], w_dt], axis=1).astype(BF16)
    widths = (pool_dim, d_ssm, conv_dim, LANES)
    return pl.pallas_call(
        functools.partial(_ab_in_kernel, widths=widths),
        grid=(bsz, seq // TM_PROJ),
        in_specs=[
            pl.BlockSpec((None, TM_PROJ, d), lambda b, l: (b, l, 0)),
            pl.BlockSpec((None, N_MOD, d), lambda b, l: (b, 0, 0)),
            _const_spec((1, d)),
            _const_spec(w.shape),
        ],
        out_specs=[pl.BlockSpec((None, TM_PROJ, wd), lambda b, l: (b, l, 0)) for wd in widths],
        out_shape=[jax.ShapeDtypeStruct((bsz, seq, wd), F32) for wd in widths],
        compiler_params=_params("arbitrary", "arbitrary"),
        name="pool_ssd_in_proj",
    )(x, modl, g.reshape(1, d), w)


def _softplus(x):
    return jnp.maximum(x, 0.0) + jnp.log1p(jnp.exp(-jnp.abs(x)))


def _mixer_kernel(u_ref, z_ref, xbc_ref, dtr_ref, pw_ref, ps_ref, cw_ref, cb_ref, dtb_ref,
                  aneg_ref, dsk_ref, ng_ref, rexp_ref, o_ref, xext, uext, st_ref, xc_ref,
                  *, pool_dim, d_ssm):
    tm = u_ref.shape[0]
    q = SSD_CHUNK
    n_groups = SSD_GROUPS
    gdim = d_ssm // n_groups
    heads_per_group = gdim // SSD_HEADDIM
    pool_gdim = pool_dim // len(POOL_WINDOWS)
    max_win = max(POOL_WINDOWS)
    lt = pl.program_id(1)

    @pl.when(lt == 0)
    def _():
        xext[0:SUBLANES, :] = jnp.zeros((SUBLANES, xext.shape[1]), F32)
        uext[0:max_win, :] = jnp.zeros((max_win, uext.shape[1]), F32)
        st_ref[...] = jnp.zeros_like(st_ref)

    xext[SUBLANES:SUBLANES + tm, :] = xbc_ref[...]
    ext = xext[...]
    acc = cb_ref[...] + cw_ref[CONV_K - 1:CONV_K, :] * xbc_ref[...]
    for k in range(CONV_K - 1):
        back = CONV_K - 1 - k
        acc = acc + cw_ref[k:k + 1, :] * pltpu.roll(ext, back, 0)[SUBLANES:, :]
    xc_ref[...] = _silu(acc)
    xext[0:SUBLANES, :] = xext[tm:tm + SUBLANES, :]

    uext[max_win:max_win + tm, :] = u_ref[...]
    tpos = (lt * tm + lax.broadcasted_iota(jnp.int32, (tm, pool_gdim), 0)).astype(F32)
    for gi, win in enumerate(POOL_WINDOWS):
        cols = slice(gi * pool_gdim, (gi + 1) * pool_gdim)
        cur = u_ref[:, cols]
        assert win & (win - 1) == 0 and win <= max_win
        ws = uext[:, cols]
        span = 1
        while span < win:
            ws = ws + pltpu.roll(ws, span, 0)
            span *= 2
        ws = ws[max_win:, :]
        count = jnp.minimum(tpos + 1.0, float(win))
        dlt = ws / count - cur
        yp = _dot(dlt.astype(BF16), pw_ref[gi]) * ps_ref[:, cols]
        o_ref[:, cols] = yp.astype(BF16)
    uext[0:max_win, :] = uext[tm:tm + max_win, :]

    row = lax.broadcasted_iota(jnp.int32, (q, q), 0)
    col = lax.broadcasted_iota(jnp.int32, (q, q), 1)
    causal = row >= col
    tril = jnp.where(causal, 1.0, 0.0).astype(BF16)
    lane = lax.broadcasted_iota(jnp.int32, (1, LANES), 1)
    lo_half = lane < SSD_HEADDIM
    rexp = rexp_ref[...]

    def expand(v):
        hi, lo = _split_bf16(v, 2)
        return _dot(hi, rexp) + _dot(lo, rexp)

    for ci in range(tm // q):
        r0 = ci * q
        xs = xc_ref[r0:r0 + q, 0:d_ssm]
        dt = _softplus(dtr_ref[r0:r0 + q, :] + dtb_ref[...])
        adt = dt * aneg_ref[...]
        acs = sum(_dot(tril, p) for p in _split_bf16(adt, 3))
        acs_t = acs.T
        a_last = acs[q - 1:q, :]
        dt_e = expand(dt)
        eacs_e = expand(jnp.exp(acs))
        wend_e = expand(jnp.exp(a_last - acs) * dt)
        xdt = xs * dt_e
        xw = xs * wend_e
        for g in range(n_groups):
            gc = slice(g * gdim, (g + 1) * gdim)
            b0 = d_ssm + g * D_STATE
            c0 = d_ssm + n_groups * D_STATE + g * D_STATE
            bm = xc_ref[r0:r0 + q, b0:b0 + D_STATE]
            cm = xc_ref[r0:r0 + q, c0:c0 + D_STATE].astype(BF16)
            cb = _dot_nt(cm, bm.astype(BF16))
            s_prev = st_ref[g]
            y_off = _dot(cm, s_prev.astype(BF16)) * eacs_e[:, gc]
            ys = []
            for pr in range(heads_per_group // 2):
                h0 = g * heads_per_group + 2 * pr
                gs = []
                for hh in range(2):
                    h = h0 + hh
                    diff = acs[:, h:h + 1] - acs_t[h:h + 1, :]
                    dec = jnp.exp(jnp.where(causal, diff, -jnp.inf))
                    gs.append((cb * dec).astype(BF16))
                xp = xdt[:, h0 * SSD_HEADDIM:(h0 + 2) * SSD_HEADDIM]
                xbd = jnp.concatenate([jnp.where(lo_half, xp, 0.0), jnp.where(lo_half, 0.0, xp)],
                                      axis=0).astype(BF16)
                ys.append(_dot(jnp.concatenate(gs, axis=1), xbd))
            y = jnp.concatenate(ys, axis=1) + y_off
            st_ref[g] = eacs_e[q - 1:q, gc] * s_prev + _dot(bm.T.astype(BF16), xw[:, gc].astype(BF16))
            y = y + dsk_ref[:, gc] * xs[:, gc]
            y = y * _silu(z_ref[r0:r0 + q, gc])
            ms = jnp.mean(y * y, axis=-1, keepdims=True)
            y = y * lax.rsqrt(ms + RMS_EPS) * ng_ref[:, gc]
            o_ref[r0:r0 + q, pool_dim + g * gdim:pool_dim + (g + 1) * gdim] = y.astype(BF16)


def _mixer_core(u, z, xbc, dtr, pool_w, pool_scale, conv_w, conv_b, dt_bias, a_log, d_skip, norm_g):
    bsz, seq, pool_dim = u.shape
    d_ssm = z.shape[-1]
    conv_dim = xbc.shape[-1]
    n_heads = dt_bias.shape[0]
    pad = LANES - n_heads
    dtb = jnp.pad(dt_bias.astype(F32), (0, pad)).reshape(1, LANES)
    aneg = jnp.pad(-jnp.exp(a_log.astype(F32)), (0, pad)).reshape(1, LANES)
    dsk = jnp.repeat(d_skip.astype(F32), SSD_HEADDIM).reshape(1, d_ssm)
    rexp = (jnp.arange(LANES)[:, None] == (jnp.arange(d_ssm) // SSD_HEADDIM)[None, :]).astype(BF16)
    consts = [pool_w.astype(BF16), pool_scale.reshape(1, pool_dim), conv_w, conv_b.reshape(1, conv_dim),
              dtb, aneg, dsk, norm_g.reshape(1, d_ssm), rexp]
    widths = (pool_dim, d_ssm, conv_dim, LANES)
    max_win = max(POOL_WINDOWS)
    return pl.pallas_call(
        functools.partial(_mixer_kernel, pool_dim=pool_dim, d_ssm=d_ssm),
        grid=(bsz, seq // TM_MIX),
        in_specs=[pl.BlockSpec((None, TM_MIX, wd), lambda b, l: (b, l, 0)) for wd in widths]
        + [_const_spec(a.shape) for a in consts],
        out_specs=pl.BlockSpec((None, TM_MIX, pool_dim + d_ssm), lambda b, l: (b, l, 0)),
        out_shape=jax.ShapeDtypeStruct((bsz, seq, pool_dim + d_ssm), BF16),
        scratch_shapes=[
            pltpu.VMEM((SUBLANES + TM_MIX, conv_dim), F32),
            pltpu.VMEM((max_win + TM_MIX, pool_dim), F32),
            pltpu.VMEM((SSD_GROUPS, D_STATE, d_ssm // SSD_GROUPS), F32),
            pltpu.VMEM((TM_MIX, conv_dim), F32),
        ],
        compiler_params=_params("arbitrary", "arbitrary"),
        name="pool_ssd_core",
    )(u, z, xbc, dtr, *consts)


def _mla_proj_kernel(x_ref, mod_ref, g_ref, pos_ref, win_ref, qg_ref, wuq_ref, kvg_ref, wk_ref,
                     wv_ref, invf_ref, q_ref, k_ref, v_ref, *, q_lora, kv_lora, scale):
    rows = x_ref.shape[0] // PROJ_SUBTILES
    n_heads = q_ref.shape[0]
    lane = lax.broadcasted_iota(jnp.int32, (1, HEAD_PAD), 1)
    first = (lane >= ROPE_LANE0) & (lane < ROPE_LANE0 + HALF_ROPE)
    second = (lane >= ROPE_LANE0 + HALF_ROPE) & (lane < ROPE_LANE0 + QK_ROPE)

    def rms(t, gain):
        return t * lax.rsqrt(jnp.mean(t * t, axis=-1, keepdims=True) + RMS_EPS) * gain

    def in_proj(i):
        r = slice(i * rows, (i + 1) * rows)
        h = _norm_mod(x_ref[r, :], g_ref[...], mod_ref[3:4, :], mod_ref[4:5, :]).astype(BF16)
        return _dot(h, win_ref[...])

    def latents(i, p):
        r = slice(i * rows, (i + 1) * rows)
        qn = rms(p[:, 0:q_lora], qg_ref[...]).astype(BF16)
        kvn = rms(p[:, q_lora:q_lora + kv_lora], kvg_ref[...]).astype(BF16)
        k_r = p[:, q_lora + kv_lora:q_lora + kv_lora + HEAD_PAD]
        ang = pos_ref[r, :] * invf_ref[...]
        sn = jnp.sin(ang)
        return qn, kvn, k_r, (jnp.cos(ang), jnp.where(first, -sn, 0.0), jnp.where(second, sn, 0.0))

    def up_proj(qn, kvn):
        return _dot(qn, wuq_ref[...]), _dot(kvn, wk_ref[...]), _dot_nt(wv_ref[...], kvn)

    def rope(t, tables, mult):
        cs, s_first, s_second = tables
        up = pltpu.roll(t, HEAD_PAD - HALF_ROPE, 1)
        down = pltpu.roll(t, HALF_ROPE, 1)
        return (t * cs + up * s_first + down * s_second) * mult

    def write(i, qf, kf, vt, k_r, tables):
        r = slice(i * rows, (i + 1) * rows)
        k_rot = rope(k_r, tables, 1.0)
        for hd in range(n_heads):
            cols = slice(hd * HEAD_PAD, (hd + 1) * HEAD_PAD)
            q_ref[hd, r, :] = rope(qf[:, cols], tables, scale).astype(BF16)
            k_ref[hd, r, :] = (kf[:, cols] + k_rot).astype(BF16)
        for hp in range(v_ref.shape[0]):
            v_ref[hp, :, r] = vt[hp * LANES:(hp + 1) * LANES, :].astype(BF16)

    ps = [in_proj(i) for i in range(PROJ_SUBTILES)]
    pending = None
    for i in range(PROJ_SUBTILES):
        qn, kvn, k_r, tables = latents(i, ps[i])
        ups = up_proj(qn, kvn)
        if pending is not None:
            write(*pending)
        pending = (i, *ups, k_r, tables)
    write(*pending)


def _mla_proj(x, modl, g, positions, w_in, q_norm_g, w_uq, kv_norm_g, w_ukv):
    bsz, seq, d = x.shape
    q_lora = w_uq.shape[0]
    kv_lora = w_ukv.shape[0]
    nh = MLA_HEADS
    w_rope = jnp.pad(w_in[:, q_lora + kv_lora:], ((0, 0), (ROPE_LANE0, HEAD_PAD - ROPE_LANE0 - QK_ROPE)))
    win = jnp.concatenate([w_in[:, :q_lora + kv_lora], w_rope], axis=1).astype(BF16)
    wuq = jnp.pad(w_uq.reshape(q_lora, nh, QK_NOPE + QK_ROPE),
                  ((0, 0), (0, 0), (0, HEAD_PAD - QK_NOPE - QK_ROPE))).reshape(q_lora, nh * HEAD_PAD)
    wkv = w_ukv.reshape(kv_lora, nh, QK_NOPE + V_DIM)
    wk = jnp.pad(wkv[:, :, :QK_NOPE], ((0, 0), (0, 0), (0, HEAD_PAD - QK_NOPE))).reshape(kv_lora, nh * HEAD_PAD)
    wv = wkv[:, :, QK_NOPE:].reshape(kv_lora, nh * V_DIM).T
    inv_freq = ROPE_THETA ** (-jnp.arange(0, QK_ROPE, 2, dtype=F32) / QK_ROPE)
    invf = jnp.zeros((HEAD_PAD,), F32).at[ROPE_LANE0:ROPE_LANE0 + QK_ROPE].set(jnp.tile(inv_freq, 2))
    pos = positions.astype(F32).reshape(bsz, seq, 1)
    scale = math.log2(math.e) / math.sqrt(QK_NOPE + QK_ROPE)
    consts = [win, q_norm_g.reshape(1, q_lora), wuq.astype(BF16), kv_norm_g.reshape(1, kv_lora),
              wk.astype(BF16), wv.astype(BF16), invf.reshape(1, HEAD_PAD)]
    n_vp = nh * V_DIM // LANES
    return pl.pallas_call(
        functools.partial(_mla_proj_kernel, q_lora=q_lora, kv_lora=kv_lora, scale=scale),
        grid=(bsz, seq // TM_PROJ),
        in_specs=[
            pl.BlockSpec((None, TM_PROJ, d), lambda b, l: (b, l, 0)),
            pl.BlockSpec((None, N_MOD, d), lambda b, l: (b, 0, 0)),
            _const_spec((1, d)),
            pl.BlockSpec((None, TM_PROJ, 1), lambda b, l: (b, l, 0)),
        ] + [_const_spec(a.shape) for a in consts],
        out_specs=[
            pl.BlockSpec((None, nh, TM_PROJ, HEAD_PAD), lambda b, l: (b, 0, l, 0)),
            pl.BlockSpec((None, nh, TM_PROJ, HEAD_PAD), lambda b, l: (b, 0, l, 0)),
            pl.BlockSpec((None, n_vp, LANES, TM_PROJ), lambda b, l: (b, 0, 0, l)),
        ],
        out_shape=[
            jax.ShapeDtypeStruct((bsz, nh, seq, HEAD_PAD), BF16),
            jax.ShapeDtypeStruct((bsz, nh, seq, HEAD_PAD), BF16),
            jax.ShapeDtypeStruct((bsz, n_vp, LANES, seq), BF16),
        ],
        compiler_params=_params("arbitrary", "arbitrary"),
        name="mla_qkv_proj",
    )(x, modl, g.reshape(1, d), pos, *consts)


def _attn_kernel(q_ref, k_ref, vt_ref, o_ref, sa_ref, sb_ref, pa_ref, pb_ref):
    seq = vt_ref.shape[2]
    n_heads = q_ref.shape[0]
    blocks = TQ // TK
    assert TQ % (2 * TK) == 0
    n_groups = TQ // LANES
    block_groups = TK // LANES
    row_minus_lane = (lax.broadcasted_iota(jnp.int32, (TK, LANES), 0)
                      - lax.broadcasted_iota(jnp.int32, (TK, LANES), 1))
    ones_rows = jnp.ones((BF16_SUBLANES, TK), BF16)

    def load_q(q0):
        return [q_ref[hh, pl.ds(q0, TQ), :] for hh in range(n_heads)]

    def scores(dst_ref, qs, k0, lo=0):
        for hh in range(n_heads):
            dst_ref[hh, :, lo:] = _dot_nt(k_ref[hh, pl.ds(k0, TK), :], qs[hh][lo:])

    def softmax(state, s_ref, p_ref, groups):
        out = []
        for hh in range(n_heads):
            m_old = state[3 * hh]
            ms, alphas = [], []
            for grp, delta in groups:
                cols = slice(grp * LANES, (grp + 1) * LANES)
                st = s_ref[hh, :, cols]
                if delta is not None:
                    st = jnp.where(row_minus_lane <= delta, st, -jnp.inf)
                m_prev = m_old[:, cols]
                m_new = jnp.maximum(m_prev, jnp.max(st, axis=0, keepdims=True))
                ms.append(m_new)
                alphas.append(jnp.exp2(m_prev - m_new))
                p_ref[hh, :, cols] = jnp.exp2(st - m_new).astype(BF16)
            out.append((jnp.concatenate(ms, axis=1), jnp.concatenate(alphas, axis=1)))
        return out

    def accumulate(state, soft, p_ref, k0, lo):
        vts = [vt_ref[hp, :, pl.ds(k0, TK)] for hp in range(n_heads // 2)]
        new = []
        for hh in range(n_heads):
            m_old, l_old, acc = state[3 * hh:3 * hh + 3]
            m_new, alpha = soft[hh]
            v_rows = vts[hh // 2][(hh % 2) * V_DIM:(hh % 2 + 1) * V_DIM, :]
            lhs = jnp.concatenate([v_rows, ones_rows], axis=0)
            pv = _dot(lhs, p_ref[hh, :, lo:])
            upd = [m_new, alpha * l_old[:, lo:] + pv[V_DIM:V_DIM + 1, :], acc[:, lo:] * alpha + pv[:V_DIM, :]]
            if lo:
                upd = [jnp.concatenate([old[:, :lo], u], axis=1) for old, u in zip((m_old, l_old, acc), upd)]
            new += upd
        return tuple(new)

    below = [(grp, None) for grp in range(n_groups)]

    def diagonal(d):
        first = d * block_groups
        return [(grp, (grp - first) * LANES if grp < first + block_groups else None)
                for grp in range(first, n_groups)]

    def q_tile(qi, carry):
        q0 = pl.multiple_of(qi * TQ, TQ)
        qs = load_q(q0)

        def pair(t, state):
            k0 = pl.multiple_of(2 * t * TK, 2 * TK)
            soft = softmax(state, sa_ref, pa_ref, below)
            scores(sa_ref, qs, k0 + 2 * TK)
            state = accumulate(state, soft, pa_ref, k0, 0)
            soft = softmax(state, sb_ref, pb_ref, below)
            scores(sb_ref, qs, k0 + 3 * TK)
            return accumulate(state, soft, pb_ref, k0 + TK, 0)

        init = (jnp.full((1, TQ), -jnp.inf, F32), jnp.zeros((1, TQ), F32),
                jnp.zeros((V_DIM, TQ), F32)) * n_heads
        state = lax.fori_loop(0, qi * (blocks // 2), pair, init)
        next_qs = load_q(pl.multiple_of(jnp.minimum(q0 + TQ, seq - TQ), TQ))
        for d in range(blocks):
            s_ref, p_ref = (sa_ref, pa_ref) if d % 2 == 0 else (sb_ref, pb_ref)
            soft = softmax(state, s_ref, p_ref, diagonal(d))
            if d + 2 < blocks:
                scores(s_ref, qs, q0 + (d + 2) * TK, (d + 2) * TK)
            else:
                scores(s_ref, next_qs, (d + 2 - blocks) * TK)
            state = accumulate(state, soft, p_ref, q0 + d * TK, d * TK)
        out_t = jnp.concatenate([state[3 * hh + 2] / state[3 * hh + 1] for hh in range(n_heads)], axis=0)
        o_ref[pl.ds(q0, TQ), :] = out_t.T.astype(BF16)
        return carry

    first_qs = load_q(0)
    scores(sa_ref, first_qs, 0)
    scores(sb_ref, first_qs, TK)
    lax.fori_loop(0, seq // TQ, q_tile, 0)


def _attention(q, k, v):
    bsz, nh, seq, _ = q.shape
    return pl.pallas_call(
        _attn_kernel,
        grid=(bsz, nh // ATTN_HEADS),
        in_specs=[
            pl.BlockSpec((None, ATTN_HEADS, seq, HEAD_PAD), lambda b, hg: (b, hg, 0, 0)),
            pl.BlockSpec((None, ATTN_HEADS, seq, HEAD_PAD), lambda b, hg: (b, hg, 0, 0)),
            pl.BlockSpec((None, ATTN_HEADS // 2, LANES, seq), lambda b, hg: (b, hg, 0, 0)),
        ],
        out_specs=pl.BlockSpec((None, seq, ATTN_HEADS * V_DIM), lambda b, hg: (b, 0, hg)),
        out_shape=jax.ShapeDtypeStruct((bsz, seq, nh * V_DIM), BF16),
        scratch_shapes=([pltpu.VMEM((ATTN_HEADS, TK, TQ), F32)] * 2
                        + [pltpu.VMEM((ATTN_HEADS, TK, TQ), BF16)] * 2),
        compiler_params=_params("arbitrary", "arbitrary"),
        name="mla_causal_attention",
    )(q, k, v)


def kernel(x, c, positions, mod_w, mod_b, norm_g, ffn_w13, ffn_w2, ab_w_in, pool_w, pool_scale, ssd_conv_w, ssd_conv_b, ssd_dt_bias, ssd_a_log, ssd_d, ssd_norm_g, ab_w_out, mla_w_in, mla_q_norm_g, mla_w_uq, mla_kv_norm_g, mla_w_ukv, mla_w_o, final_norm_g):
    depth = mod_w.shape[0]
    mod = _modulation(c, mod_w, mod_b)
    w13, w2 = ffn_w13.astype(BF16), ffn_w2.astype(BF16)
    for i in range(depth):
        modl = mod[i]
        j = i // 2
        x = _ffn(x, modl, norm_g[i, 0], w13, w2, i, sub=0)
        if i % 2 == 0:
            pool_dim = pool_scale.shape[-1]
            d_ssm = ssd_norm_g.shape[-1]
            conv_dim = ssd_conv_b.shape[-1]
            n_heads = ssd_dt_bias.shape[-1]
            u, z, xbc, dtr = _ab_in_proj(x, modl, norm_g[i, 1], ab_w_in[j], pool_dim, d_ssm, conv_dim, n_heads)
            y = _mixer_core(u, z, xbc, dtr, pool_w[j], pool_scale[j], ssd_conv_w[j], ssd_conv_b[j],
                            ssd_dt_bias[j], ssd_a_log[j], ssd_d[j], ssd_norm_g[j])
            mixer = (y, ab_w_out[j])
        else:
            q, k, v = _mla_proj(x, modl, norm_g[i, 1], positions, mla_w_in[j], mla_q_norm_g[j],
                                mla_w_uq[j], mla_kv_norm_g[j], mla_w_ukv[j])
            mixer = (_attention(q, k, v), mla_w_o[j])
        last = i == depth - 1
        x = _ffn(x, modl, norm_g[i, 2], w13, w2, i, sub=2, mixer=mixer,
                 final_g=final_norm_g if last else None)
    return x
```

```python
import functools
import math

import jax
import jax.numpy as jnp
from jax import lax
from jax.experimental import pallas as pl
from jax.experimental.pallas import tpu as pltpu

F32 = jnp.float32
BF16 = jnp.bfloat16

RMS_EPS = 1e-6
FFN_RES = 0.5
POOL_WINDOWS = (2, 4, 8, 16)
SSD_HEADDIM = 64
SSD_GROUPS = 2
D_STATE = 128
CONV_K = 4
SSD_CHUNK = 128
MLA_HEADS = 16
QK_NOPE = 64
QK_ROPE = 32
V_DIM = 64
ROPE_THETA = 10000.0
N_MOD = 9

LANES = 128
SUBLANES = 8
BF16_SUBLANES = 16
VMEM_LIMIT_BYTES = 56 * 1024 * 1024

TM_FFN = 1024
FFN_CHUNK = 256
FFN_SUBTILES = 2
TM_PROJ = 512
PROJ_SUBTILES = 2
TM_MIX = 512
TM_MOD = 2304
TQ = 1024
TK = 256
ATTN_HEADS = 4
HALF_ROPE = QK_ROPE // 2
HEAD_PAD = 128
ROPE_LANE0 = QK_NOPE


def _params(*sem):
    return pltpu.CompilerParams(dimension_semantics=sem, vmem_limit_bytes=VMEM_LIMIT_BYTES)


def _const_spec(shape):
    nd = len(shape)
    return pl.BlockSpec(shape, lambda *_: (0,) * nd, pipeline_mode=pl.Buffered(1))


def _silu(a):
    half = 0.5 * a
    return half + half * jnp.tanh(half)


def _dot(a, b):
    return jnp.dot(a, b, preferred_element_type=F32)


def _dot_nt(a, b):
    return lax.dot_general(a, b, (((1,), (1,)), ((), ())), preferred_element_type=F32)


def _norm_mod(x, g, shift, scale):
    ms = jnp.mean(x * x, axis=-1, keepdims=True)
    y = x * lax.rsqrt(ms + RMS_EPS) * g
    return y * (1.0 + scale) + shift


def _split_bf16(x, parts):
    out = []
    r = x
    for _ in range(parts):
        p = r.astype(BF16)
        out.append(p)
        r = r - p.astype(F32)
    return out


def _mod_kernel(c_ref, w_ref, b_ref, o_ref):
    ca = _split_bf16(_silu(c_ref[...]), 2)
    w = _split_bf16(w_ref[...], 2)
    acc = b_ref[...]
    for i, j in ((1, 0), (0, 1), (0, 0)):
        acc = acc + _dot(ca[i], w[j])
    o_ref[...] = acc


def _modulation(c, mod_w, mod_b):
    depth, d, n = mod_w.shape
    bsz = c.shape[0]
    out = pl.pallas_call(
        _mod_kernel,
        grid=(depth, n // TM_MOD),
        in_specs=[
            pl.BlockSpec((bsz, d), lambda i, j: (0, 0)),
            pl.BlockSpec((None, d, TM_MOD), lambda i, j: (i, 0, j)),
            pl.BlockSpec((None, 1, TM_MOD), lambda i, j: (i, 0, j)),
        ],
        out_specs=pl.BlockSpec((None, bsz, TM_MOD), lambda i, j: (i, 0, j)),
        out_shape=jax.ShapeDtypeStruct((depth, bsz, n), F32),
        compiler_params=_params("arbitrary", "arbitrary"),
        name="adaln_modulation",
    )(c, mod_w, mod_b.reshape(depth, 1, n))
    return out.reshape(depth, bsz, N_MOD, d)


def _ffn_kernel(*refs, sub, d_ff, final, mixed):
    x_ref, mod_ref, g_ref, w13_ref, w2_ref = refs[:5]
    o_ref, h_ref, acc_ref = refs[-3:]
    extra = refs[5:-3]
    shift = mod_ref[3 * sub:3 * sub + 1, :]
    scale = mod_ref[3 * sub + 1:3 * sub + 2, :]
    gate = mod_ref[3 * sub + 2:3 * sub + 3, :]
    rows = x_ref.shape[0] // FFN_SUBTILES

    def residual_in(i):
        x = x_ref[i * rows:(i + 1) * rows, :]
        if mixed:
            y_ref, wo_ref = extra[:2]
            x = x + mod_ref[5:6, :] * _dot(y_ref[i * rows:(i + 1) * rows, :], wo_ref[...])
        return x

    def chunk(i, c0):
        h = h_ref[i]
        a = _dot(h, w13_ref[:, c0:c0 + FFN_CHUNK])
        b = _dot(h, w13_ref[:, d_ff + c0:d_ff + c0 + FFN_CHUNK])
        part = _dot((_silu(a) * b).astype(BF16), w2_ref[c0:c0 + FFN_CHUNK, :])
        if c0 == 0:
            acc_ref[i] = part
        else:
            acc_ref[i] += part

    def finish(i, x):
        y = x + (FFN_RES * gate) * acc_ref[i]
        if final:
            fg_ref = extra[-1]
            ms = jnp.mean(y * y, axis=-1, keepdims=True)
            y = y * lax.rsqrt(ms + RMS_EPS) * fg_ref[...]
        o_ref[i * rows:(i + 1) * rows, :] = y

    starts = list(range(0, d_ff, FFN_CHUNK))
    xs = []
    for i in range(FFN_SUBTILES):
        xs.append(residual_in(i))
        h_ref[i] = _norm_mod(xs[i], g_ref[...], shift, scale).astype(BF16)
        chunk(i, starts[0])
    for c0 in starts[1:-1]:
        for i in range(FFN_SUBTILES):
            chunk(i, c0)
    for i in range(FFN_SUBTILES):
        chunk(i, starts[-1])
        finish(i, xs[i])


def _ffn(x, modl, g, w13, w2, layer, sub, mixer=None, final_g=None):
    bsz, seq, d = x.shape
    d_ff = w2.shape[-2]
    assert d_ff % FFN_CHUNK == 0
    which = sub // 2

    def weight_spec(rows, cols):
        return pl.BlockSpec((None, None, rows, cols), lambda b, l: (layer, which, 0, 0),
                            pipeline_mode=pl.Buffered(1))

    in_specs = [
        pl.BlockSpec((None, TM_FFN, d), lambda b, l: (b, l, 0)),
        pl.BlockSpec((None, N_MOD, d), lambda b, l: (b, 0, 0)),
        _const_spec((1, d)),
        weight_spec(d, 2 * d_ff),
        weight_spec(d_ff, d),
    ]
    args = [x, modl, g.reshape(1, d), w13, w2]
    if mixer is not None:
        y, w_out = mixer
        k = y.shape[-1]
        in_specs += [pl.BlockSpec((None, TM_FFN, k), lambda b, l: (b, l, 0)), _const_spec((k, d))]
        args += [y, w_out.astype(BF16)]
    if final_g is not None:
        in_specs.append(_const_spec((1, d)))
        args.append(final_g.reshape(1, d))
    return pl.pallas_call(
        functools.partial(_ffn_kernel, sub=sub, d_ff=d_ff, final=final_g is not None,
                          mixed=mixer is not None),
        grid=(bsz, seq // TM_FFN),
        in_specs=in_specs,
        out_specs=pl.BlockSpec((None, TM_FFN, d), lambda b, l: (b, l, 0)),
        out_shape=jax.ShapeDtypeStruct((bsz, seq, d), F32),
        scratch_shapes=[pltpu.VMEM((FFN_SUBTILES, TM_FFN // FFN_SUBTILES, d), BF16),
                        pltpu.VMEM((FFN_SUBTILES, TM_FFN // FFN_SUBTILES, d), F32)],
        compiler_params=_params("arbitrary", "arbitrary"),
        name="swiglu_ffn",
    )(*args)


def _ab_in_kernel(x_ref, mod_ref, g_ref, w_ref, u_ref, z_ref, xbc_ref, dt_ref, *, widths):
    rows = x_ref.shape[0] // PROJ_SUBTILES
    ps = []
    for i in range(PROJ_SUBTILES):
        r = slice(i * rows, (i + 1) * rows)
        h = _norm_mod(x_ref[r, :], g_ref[...], mod_ref[3:4, :], mod_ref[4:5, :]).astype(BF16)
        ps.append(_dot(h, w_ref[...]))
    for i, p in enumerate(ps):
        r = slice(i * rows, (i + 1) * rows)
        c0 = 0
        for ref, wd in zip((u_ref, z_ref, xbc_ref, dt_ref), widths):
            ref[r, :] = p[:, c0:c0 + wd]
            c0 += wd


def _ab_in_proj(x, modl, g, w_in, pool_dim, d_ssm, conv_dim, n_heads):
    bsz, seq, d = x.shape
    main = pool_dim + d_ssm + conv_dim
    w_dt = jnp.pad(w_in[:, main:main + n_heads], ((0, 0), (0, LANES - n_heads)))
    w = jnp.concatenate([w_in[:, :main], w_dt], axis=1).astype(BF16)
    widths = (pool_dim, d_ssm, conv_dim, LANES)
    return pl.pallas_call(
        functools.partial(_ab_in_kernel, widths=widths),
        grid=(bsz, seq // TM_PROJ),
        in_specs=[
            pl.BlockSpec((None, TM_PROJ, d), lambda b, l: (b, l, 0)),
            pl.BlockSpec((None, N_MOD, d), lambda b, l: (b, 0, 0)),
            _const_spec((1, d)),
            _const_spec(w.shape),
        ],
        out_specs=[pl.BlockSpec((None, TM_PROJ, wd), lambda b, l: (b, l, 0)) for wd in widths],
        out_shape=[jax.ShapeDtypeStruct((bsz, seq, wd), F32) for wd in widths],
        compiler_params=_params("arbitrary", "arbitrary"),
        name="pool_ssd_in_proj",
    )(x, modl, g.reshape(1, d), w)


def _softplus(x):
    return jnp.maximum(x, 0.0) + jnp.log1p(jnp.exp(-jnp.abs(x)))


def _mixer_kernel(u_ref, z_ref, xbc_ref, dtr_ref, pw_ref, ps_ref, cw_ref, cb_ref, dtb_ref,
                  aneg_ref, dsk_ref, ng_ref, rexp_ref, o_ref, xext, uext, st_ref, xc_ref,
                  *, pool_dim, d_ssm):
    tm = u_ref.shape[0]
    q = SSD_CHUNK
    n_groups = SSD_GROUPS
    gdim = d_ssm // n_groups
    heads_per_group = gdim // SSD_HEADDIM
    pool_gdim = pool_dim // len(POOL_WINDOWS)
    max_win = max(POOL_WINDOWS)
    lt = pl.program_id(1)

    @pl.when(lt == 0)
    def _():
        xext[0:SUBLANES, :] = jnp.zeros((SUBLANES, xext.shape[1]), F32)
        uext[0:max_win, :] = jnp.zeros((max_win, uext.shape[1]), F32)
        st_ref[...] = jnp.zeros_like(st_ref)

    xext[SUBLANES:SUBLANES + tm, :] = xbc_ref[...]
    ext = xext[...]
    acc = cb_ref[...] + cw_ref[CONV_K - 1:CONV_K, :] * xbc_ref[...]
    for k in range(CONV_K - 1):
        back = CONV_K - 1 - k
        acc = acc + cw_ref[k:k + 1, :] * pltpu.roll(ext, back, 0)[SUBLANES:, :]
    xc_ref[...] = _silu(acc)
    xext[0:SUBLANES, :] = xext[tm:tm + SUBLANES, :]

    uext[max_win:max_win + tm, :] = u_ref[...]
    tpos = (lt * tm + lax.broadcasted_iota(jnp.int32, (tm, pool_gdim), 0)).astype(F32)
    for gi, win in enumerate(POOL_WINDOWS):
        cols = slice(gi * pool_gdim, (gi + 1) * pool_gdim)
        cur = u_ref[:, cols]
        assert win & (win - 1) == 0 and win <= max_win
        ws = uext[:, cols]
        span = 1
        while span < win:
            ws = ws + pltpu.roll(ws, span, 0)
            span *= 2
        ws = ws[max_win:, :]
        count = jnp.minimum(tpos + 1.0, float(win))
        dlt = ws / count - cur
        yp = _dot(dlt.astype(BF16), pw_ref[gi]) * ps_ref[:, cols]
        o_ref[:, cols] = yp.astype(BF16)
    uext[0:max_win, :] = uext[tm:tm + max_win, :]

    row = lax.broadcasted_iota(jnp.int32, (q, q), 0)
    col = lax.broadcasted_iota(jnp.int32, (q, q), 1)
    causal = row >= col
    tril = jnp.where(causal, 1.0, 0.0).astype(BF16)
    lane = lax.broadcasted_iota(jnp.int32, (1, LANES), 1)
    lo_half = lane < SSD_HEADDIM
    rexp = rexp_ref[...]

    def expand(v):
        hi, lo = _split_bf16(v, 2)
        return _dot(hi, rexp) + _dot(lo, rexp)

    for ci in range(tm // q):
        r0 = ci * q
        xs = xc_ref[r0:r0 + q, 0:d_ssm]
        dt = _softplus(dtr_ref[r0:r0 + q, :] + dtb_ref[...])
        adt = dt * aneg_ref[...]
        acs = sum(_dot(tril, p) for p in _split_bf16(adt, 3))
        acs_t = acs.T
        a_last = acs[q - 1:q, :]
        dt_e = expand(dt)
        eacs_e = expand(jnp.exp(acs))
        wend_e = expand(jnp.exp(a_last - acs) * dt)
        xdt = xs * dt_e
        xw = xs * wend_e
        for g in range(n_groups):
            gc = slice(g * gdim, (g + 1) * gdim)
            b0 = d_ssm + g * D_STATE
            c0 = d_ssm + n_groups * D_STATE + g * D_STATE
            bm = xc_ref[r0:r0 + q, b0:b0 + D_STATE]
            cm = xc_ref[r0:r0 + q, c0:c0 + D_STATE].astype(BF16)
            cb = _dot_nt(cm, bm.astype(BF16))
            s_prev = st_ref[g]
            y_off = _dot(cm, s_prev.astype(BF16)) * eacs_e[:, gc]
            ys = []
            for pr in range(heads_per_group // 2):
                h0 = g * heads_per_group + 2 * pr
                gs = []
                for hh in range(2):
                    h = h0 + hh
                    diff = acs[:, h:h + 1] - acs_t[h:h + 1, :]
                    dec = jnp.exp(jnp.where(causal, diff, -jnp.inf))
                    gs.append((cb * dec).astype(BF16))
                xp = xdt[:, h0 * SSD_HEADDIM:(h0 + 2) * SSD_HEADDIM]
                xbd = jnp.concatenate([jnp.where(lo_half, xp, 0.0), jnp.where(lo_half, 0.0, xp)],
                                      axis=0).astype(BF16)
                ys.append(_dot(jnp.concatenate(gs, axis=1), xbd))
            y = jnp.concatenate(ys, axis=1) + y_off
            st_ref[g] = eacs_e[q - 1:q, gc] * s_prev + _dot(bm.T.astype(BF16), xw[:, gc].astype(BF16))
            y = y + dsk_ref[:, gc] * xs[:, gc]
            y = y * _silu(z_ref[r0:r0 + q, gc])
            ms = jnp.mean(y * y, axis=-1, keepdims=True)
            y = y * lax.rsqrt(ms + RMS_EPS) * ng_ref[:, gc]
            o_ref[r0:r0 + q, pool_dim + g * gdim:pool_dim + (g + 1) * gdim] = y.astype(BF16)


def _mixer_core(u, z, xbc, dtr, pool_w, pool_scale, conv_w, conv_b, dt_bias, a_log, d_skip, norm_g):
    bsz, seq, pool_dim = u.shape
    d_ssm = z.shape[-1]
    conv_dim = xbc.shape[-1]
    n_heads = dt_bias.shape[0]
    pad = LANES - n_heads
    dtb = jnp.pad(dt_bias.astype(F32), (0, pad)).reshape(1, LANES)
    aneg = jnp.pad(-jnp.exp(a_log.astype(F32)), (0, pad)).reshape(1, LANES)
    dsk = jnp.repeat(d_skip.astype(F32), SSD_HEADDIM).reshape(1, d_ssm)
    rexp = (jnp.arange(LANES)[:, None] == (jnp.arange(d_ssm) // SSD_HEADDIM)[None, :]).astype(BF16)
    consts = [pool_w.astype(BF16), pool_scale.reshape(1, pool_dim), conv_w, conv_b.reshape(1, conv_dim),
              dtb, aneg, dsk, norm_g.reshape(1, d_ssm), rexp]
    widths = (pool_dim, d_ssm, conv_dim, LANES)
    max_win = max(POOL_WINDOWS)
    return pl.pallas_call(
        functools.partial(_mixer_kernel, pool_dim=pool_dim, d_ssm=d_ssm),
        grid=(bsz, seq // TM_MIX),
        in_specs=[pl.BlockSpec((None, TM_MIX, wd), lambda b, l: (b, l, 0)) for wd in widths]
        + [_const_spec(a.shape) for a in consts],
        out_specs=pl.BlockSpec((None, TM_MIX, pool_dim + d_ssm), lambda b, l: (b, l, 0)),
        out_shape=jax.ShapeDtypeStruct((bsz, seq, pool_dim + d_ssm), BF16),
        scratch_shapes=[
            pltpu.VMEM((SUBLANES + TM_MIX, conv_dim), F32),
            pltpu.VMEM((max_win + TM_MIX, pool_dim), F32),
            pltpu.VMEM((SSD_GROUPS, D_STATE, d_ssm // SSD_GROUPS), F32),
            pltpu.VMEM((TM_MIX, conv_dim), F32),
        ],
        compiler_params=_params("arbitrary", "arbitrary"),
        name="pool_ssd_core",
    )(u, z, xbc, dtr, *consts)


def _mla_proj_kernel(x_ref, mod_ref, g_ref, pos_ref, win_ref, qg_ref, wuq_ref, kvg_ref, wk_ref,
                     wv_ref, invf_ref, q_ref, k_ref, v_ref, *, q_lora, kv_lora, scale):
    rows = x_ref.shape[0] // PROJ_SUBTILES
    n_heads = q_ref.shape[0]
    lane = lax.broadcasted_iota(jnp.int32, (1, HEAD_PAD), 1)
    first = (lane >= ROPE_LANE0) & (lane < ROPE_LANE0 + HALF_ROPE)
    second = (lane >= ROPE_LANE0 + HALF_ROPE) & (lane < ROPE_LANE0 + QK_ROPE)

    def rms(t, gain):
        return t * lax.rsqrt(jnp.mean(t * t, axis=-1, keepdims=True) + RMS_EPS) * gain

    def in_proj(i):
        r = slice(i * rows, (i + 1) * rows)
        h = _norm_mod(x_ref[r, :], g_ref[...], mod_ref[3:4, :], mod_ref[4:5, :]).astype(BF16)
        return _dot(h, win_ref[...])

    def latents(i, p):
        r = slice(i * rows, (i + 1) * rows)
        qn = rms(p[:, 0:q_lora], qg_ref[...]).astype(BF16)
        kvn = rms(p[:, q_lora:q_lora + kv_lora], kvg_ref[...]).astype(BF16)
        k_r = p[:, q_lora + kv_lora:q_lora + kv_lora + HEAD_PAD]
        ang = pos_ref[r, :] * invf_ref[...]
        sn = jnp.sin(ang)
        return qn, kvn, k_r, (jnp.cos(ang), jnp.where(first, -sn, 0.0), jnp.where(second, sn, 0.0))

    def up_proj(qn, kvn):
        return _dot(qn, wuq_ref[...]), _dot(kvn, wk_ref[...]), _dot_nt(wv_ref[...], kvn)

    def rope(t, tables, mult):
        cs, s_first, s_second = tables
        up = pltpu.roll(t, HEAD_PAD - HALF_ROPE, 1)
        down = pltpu.roll(t, HALF_ROPE, 1)
        return (t * cs + up * s_first + down * s_second) * mult

    def write(i, qf, kf, vt, k_r, tables):
        r = slice(i * rows, (i + 1) * rows)
        k_rot = rope(k_r, tables, 1.0)
        for hd in range(n_heads):
            cols = slice(hd * HEAD_PAD, (hd + 1) * HEAD_PAD)
            q_ref[hd, r, :] = rope(qf[:, cols], tables, scale).astype(BF16)
            k_ref[hd, r, :] = (kf[:, cols] + k_rot).astype(BF16)
        for hp in range(v_ref.shape[0]):
            v_ref[hp, :, r] = vt[hp * LANES:(hp + 1) * LANES, :].astype(BF16)

    ps = [in_proj(i) for i in range(PROJ_SUBTILES)]
    pending = None
    for i in range(PROJ_SUBTILES):
        qn, kvn, k_r, tables = latents(i, ps[i])
        ups = up_proj(qn, kvn)
        if pending is not None:
            write(*pending)
        pending = (i, *ups, k_r, tables)
    write(*pending)


def _mla_proj(x, modl, g, positions, w_in, q_norm_g, w_uq, kv_norm_g, w_ukv):
    bsz, seq, d = x.shape
    q_lora = w_uq.shape[0]
    kv_lora = w_ukv.shape[0]
    nh = MLA_HEADS
    w_rope = jnp.pad(w_in[:, q_lora + kv_lora:], ((0, 0), (ROPE_LANE0, HEAD_PAD - ROPE_LANE0 - QK_ROPE)))
    win = jnp.concatenate([w_in[:, :q_lora + kv_lora], w_rope], axis=1).astype(BF16)
    wuq = jnp.pad(w_uq.reshape(q_lora, nh, QK_NOPE + QK_ROPE),
                  ((0, 0), (0, 0), (0, HEAD_PAD - QK_NOPE - QK_ROPE))).reshape(q_lora, nh * HEAD_PAD)
    wkv = w_ukv.reshape(kv_lora, nh, QK_NOPE + V_DIM)
    wk = jnp.pad(wkv[:, :, :QK_NOPE], ((0, 0), (0, 0), (0, HEAD_PAD - QK_NOPE))).reshape(kv_lora, nh * HEAD_PAD)
    wv = wkv[:, :, QK_NOPE:].reshape(kv_lora, nh * V_DIM).T
    inv_freq = ROPE_THETA ** (-jnp.arange(0, QK_ROPE, 2, dtype=F32) / QK_ROPE)
    invf = jnp.zeros((HEAD_PAD,), F32).at[ROPE_LANE0:ROPE_LANE0 + QK_ROPE].set(jnp.tile(inv_freq, 2))
    pos = positions.astype(F32).reshape(bsz, seq, 1)
    scale = math.log2(math.e) / math.sqrt(QK_NOPE + QK_ROPE)
    consts = [win, q_norm_g.reshape(1, q_lora), wuq.astype(BF16), kv_norm_g.reshape(1, kv_lora),
              wk.astype(BF16), wv.astype(BF16), invf.reshape(1, HEAD_PAD)]
    n_vp = nh * V_DIM // LANES
    return pl.pallas_call(
        functools.partial(_mla_proj_kernel, q_lora=q_lora, kv_lora=kv_lora, scale=scale),
        grid=(bsz, seq // TM_PROJ),
        in_specs=[
            pl.BlockSpec((None, TM_PROJ, d), lambda b, l: (b, l, 0)),
            pl.BlockSpec((None, N_MOD, d), lambda b, l: (b, 0, 0)),
            _const_spec((1, d)),
            pl.BlockSpec((None, TM_PROJ, 1), lambda b, l: (b, l, 0)),
        ] + [_const_spec(a.shape) for a in consts],
        out_specs=[
            pl.BlockSpec((None, nh, TM_PROJ, HEAD_PAD), lambda b, l: (b, 0, l, 0)),
            pl.BlockSpec((None, nh, TM_PROJ, HEAD_PAD), lambda b, l: (b, 0, l, 0)),
            pl.BlockSpec((None, n_vp, LANES, TM_PROJ), lambda b, l: (b, 0, 0, l)),
        ],
        out_shape=[
            jax.ShapeDtypeStruct((bsz, nh, seq, HEAD_PAD), BF16),
            jax.ShapeDtypeStruct((bsz, nh, seq, HEAD_PAD), BF16),
            jax.ShapeDtypeStruct((bsz, n_vp, LANES, seq), BF16),
        ],
        compiler_params=_params("arbitrary", "arbitrary"),
        name="mla_qkv_proj",
    )(x, modl, g.reshape(1, d), pos, *consts)


def _attn_kernel(q_ref, k_ref, vt_ref, o_ref, sa_ref, sb_ref, pa_ref, pb_ref):
    seq = vt_ref.shape[2]
    n_heads = q_ref.shape[0]
    blocks = TQ // TK
    assert TQ % (2 * TK) == 0
    n_groups = TQ // LANES
    block_groups = TK // LANES
    row_minus_lane = (lax.broadcasted_iota(jnp.int32, (TK, LANES), 0)
                      - lax.broadcasted_iota(jnp.int32, (TK, LANES), 1))
    ones_rows = jnp.ones((BF16_SUBLANES, TK), BF16)

    def load_q(q0):
        return [q_ref[hh, pl.ds(q0, TQ), :] for hh in range(n_heads)]

    def scores(dst_ref, qs, k0, lo=0):
        for hh in range(n_heads):
            dst_ref[hh, :, lo:] = _dot_nt(k_ref[hh, pl.ds(k0, TK), :], qs[hh][lo:])

    def softmax(state, s_ref, p_ref, groups):
        out = []
        for hh in range(n_heads):
            m_old = state[3 * hh]
            ms, alphas = [], []
            for grp, delta in groups:
                cols = slice(grp * LANES, (grp + 1) * LANES)
                st = s_ref[hh, :, cols]
                if delta is not None:
                    st = jnp.where(row_minus_lane <= delta, st, -jnp.inf)
                m_prev = m_old[:, cols]
                m_new = jnp.maximum(m_prev, jnp.max(st, axis=0, keepdims=True))
                ms.append(m_new)
                alphas.append(jnp.exp2(m_prev - m_new))
                p_ref[hh, :, cols] = jnp.exp2(st - m_new).astype(BF16)
            out.append((jnp.concatenate(ms, axis=1), jnp.concatenate(alphas, axis=1)))
        return out

    def accumulate(state, soft, p_ref, k0, lo):
        vts = [vt_ref[hp, :, pl.ds(k0, TK)] for hp in range(n_heads // 2)]
        new = []
        for hh in range(n_heads):
            m_old, l_old, acc = state[3 * hh:3 * hh + 3]
            m_new, alpha = soft[hh]
            v_rows = vts[hh // 2][(hh % 2) * V_DIM:(hh % 2 + 1) * V_DIM, :]
            lhs = jnp.concatenate([v_rows, ones_rows], axis=0)
            pv = _dot(lhs, p_ref[hh, :, lo:])
            upd = [m_new, alpha * l_old[:, lo:] + pv[V_DIM:V_DIM + 1, :], acc[:, lo:] * alpha + pv[:V_DIM, :]]
            if lo:
                upd = [jnp.concatenate([old[:, :lo], u], axis=1) for old, u in zip((m_old, l_old, acc), upd)]
            new += upd
        return tuple(new)

    below = [(grp, None) for grp in range(n_groups)]

    def diagonal(d):
        first = d * block_groups
        return [(grp, (grp - first) * LANES if grp < first + block_groups else None)
                for grp in range(first, n_groups)]

    def q_tile(qi, carry):
        q0 = pl.multiple_of(qi * TQ, TQ)
        qs = load_q(q0)

        def pair(t, state):
            k0 = pl.multiple_of(2 * t * TK, 2 * TK)
            soft = softmax(state, sa_ref, pa_ref, below)
            scores(sa_ref, qs, k0 + 2 * TK)
            state = accumulate(state, soft, pa_ref, k0, 0)
            soft = softmax(state, sb_ref, pb_ref, below)
            scores(sb_ref, qs, k0 + 3 * TK)
            return accumulate(state, soft, pb_ref, k0 + TK, 0)

        init = (jnp.full((1, TQ), -jnp.inf, F32), jnp.zeros((1, TQ), F32),
                jnp.zeros((V_DIM, TQ), F32)) * n_heads
        state = lax.fori_loop(0, qi * (blocks // 2), pair, init)
        next_qs = load_q(pl.multiple_of(jnp.minimum(q0 + TQ, seq - TQ), TQ))
        for d in range(blocks):
            s_ref, p_ref = (sa_ref, pa_ref) if d % 2 == 0 else (sb_ref, pb_ref)
            soft = softmax(state, s_ref, p_ref, diagonal(d))
            if d + 2 < blocks:
                scores(s_ref, qs, q0 + (d + 2) * TK, (d + 2) * TK)
            else:
                scores(s_ref, next_qs, (d + 2 - blocks) * TK)
            state = accumulate(state, soft, p_ref, q0 + d * TK, d * TK)
        out_t = jnp.concatenate([state[3 * hh + 2] / state[3 * hh + 1] for hh in range(n_heads)], axis=0)
        o_ref[pl.ds(q0, TQ), :] = out_t.T.astype(BF16)
        return carry

    first_qs = load_q(0)
    scores(sa_ref, first_qs, 0)
    scores(sb_ref, first_qs, TK)
    lax.fori_loop(0, seq // TQ, q_tile, 0)


def _attention(q, k, v):
    bsz, nh, seq, _ = q.shape
    return pl.pallas_call(
        _attn_kernel,
        grid=(bsz, nh // ATTN_HEADS),
        in_specs=[
            pl.BlockSpec((None, ATTN_HEADS, seq, HEAD_PAD), lambda b, hg: (b, hg, 0, 0)),
            pl.BlockSpec((None, ATTN_HEADS, seq, HEAD_PAD), lambda b, hg: (b, hg, 0, 0)),
            pl.BlockSpec((None, ATTN_HEADS // 2, LANES, seq), lambda b, hg: (b, hg, 0, 0)),
        ],
        out_specs=pl.BlockSpec((None, seq, ATTN_HEADS * V_DIM), lambda b, hg: (b, 0, hg)),
        out_shape=jax.ShapeDtypeStruct((bsz, seq, nh * V_DIM), BF16),
        scratch_shapes=([pltpu.VMEM((ATTN_HEADS, TK, TQ), F32)] * 2
                        + [pltpu.VMEM((ATTN_HEADS, TK, TQ), BF16)] * 2),
        compiler_params=_params("arbitrary", "arbitrary"),
        name="mla_causal_attention",
    )(q, k, v)


def kernel(x, c, positions, mod_w, mod_b, norm_g, ffn_w13, ffn_w2, ab_w_in, pool_w, pool_scale, ssd_conv_w, ssd_conv_b, ssd_dt_bias, ssd_a_log, ssd_d, ssd_norm_g, ab_w_out, mla_w_in, mla_q_norm_g, mla_w_uq, mla_kv_norm_g, mla_w_ukv, mla_w_o, final_norm_g):
    depth = mod_w.shape[0]
    mod = _modulation(c, mod_w, mod_b)
    w13, w2 = ffn_w13.astype(BF16), ffn_w2.astype(BF16)
    for i in range(depth):
        modl = mod[i]
        j = i // 2
        x = _ffn(x, modl, norm_g[i, 0], w13, w2, i, sub=0)
        if i % 2 == 0:
            pool_dim = pool_scale.shape[-1]
            d_ssm = ssd_norm_g.shape[-1]
            conv_dim = ssd_conv_b.shape[-1]
            n_heads = ssd_dt_bias.shape[-1]
            u, z, xbc, dtr = _ab_in_proj(x, modl, norm_g[i, 1], ab_w_in[j], pool_dim, d_ssm, conv_dim, n_heads)
            y = _mixer_core(u, z, xbc, dtr, pool_w[j], pool_scale[j], ssd_conv_w[j], ssd_conv_b[j],
                            ssd_dt_bias[j], ssd_a_log[j], ssd_d[j], ssd_norm_g[j])
            mixer = (y, ab_w_out[j])
        else:
            q, k, v = _mla_proj(x, modl, norm_g[i, 1], positions, mla_w_in[j], mla_q_norm_g[j],
                                mla_w_uq[j], mla_kv_norm_g[j], mla_w_ukv[j])
            mixer = (_attention(q, k, v), mla_w_o[j])
        last = i == depth - 1
        x = _ffn(x, modl, norm_g[i, 2], w13, w2, i, sub=2, mixer=mixer,
                 final_g=final_norm_g if last else None)
    return x
```

```python
import functools
import math

import jax
import jax.numpy as jnp
from jax import lax
from jax.experimental import pallas as pl
from jax.experimental.pallas import tpu as pltpu

F32 = jnp.float32
BF16 = jnp.bfloat16

RMS_EPS = 1e-6
FFN_RES = 0.5
POOL_WINDOWS = (2, 4, 8, 16)
SSD_HEADDIM = 64
SSD_GROUPS = 2
D_STATE = 128
CONV_K = 4
SSD_CHUNK = 128
MLA_HEADS = 16
QK_NOPE = 64
QK_ROPE = 32
V_DIM = 64
ROPE_THETA = 10000.0
N_MOD = 9

LANES = 128
SUBLANES = 8
BF16_SUBLANES = 16
VMEM_LIMIT_BYTES = 56 * 1024 * 1024

TM_FFN = 1024
FFN_CHUNK = 256
FFN_SUBTILES = 2
TM_PROJ = 512
PROJ_SUBTILES = 2
TM_MIX = 1024
TM_MOD = 2304
TQ = 1024
TK = 256
ATTN_HEADS = 4
HALF_ROPE = QK_ROPE // 2
HEAD_PAD = 128
ROPE_LANE0 = QK_NOPE


def _params(*sem):
    return pltpu.CompilerParams(dimension_semantics=sem, vmem_limit_bytes=VMEM_LIMIT_BYTES)


def _const_spec(shape):
    nd = len(shape)
    return pl.BlockSpec(shape, lambda *_: (0,) * nd, pipeline_mode=pl.Buffered(1))


def _silu(a):
    half = 0.5 * a
    return half + half * jnp.tanh(half)


def _dot(a, b):
    return jnp.dot(a, b, preferred_element_type=F32)


def _dot_nt(a, b):
    return lax.dot_general(a, b, (((1,), (1,)), ((), ())), preferred_element_type=F32)


def _norm_mod(x, g, shift, scale):
    ms = jnp.mean(x * x, axis=-1, keepdims=True)
    y = x * lax.rsqrt(ms + RMS_EPS) * g
    return y * (1.0 + scale) + shift


def _split_bf16(x, parts):
    out = []
    r = x
    for _ in range(parts):
        p = r.astype(BF16)
        out.append(p)
        r = r - p.astype(F32)
    return out


def _mod_kernel(c_ref, w_ref, b_ref, o_ref):
    ca = _split_bf16(_silu(c_ref[...]), 2)
    w = _split_bf16(w_ref[...], 2)
    acc = b_ref[...]
    for i, j in ((1, 0), (0, 1), (0, 0)):
        acc = acc + _dot(ca[i], w[j])
    o_ref[...] = acc


def _modulation(c, mod_w, mod_b):
    depth, d, n = mod_w.shape
    bsz = c.shape[0]
    out = pl.pallas_call(
        _mod_kernel,
        grid=(depth, n // TM_MOD),
        in_specs=[
            pl.BlockSpec((bsz, d), lambda i, j: (0, 0)),
            pl.BlockSpec((None, d, TM_MOD), lambda i, j: (i, 0, j)),
            pl.BlockSpec((None, 1, TM_MOD), lambda i, j: (i, 0, j)),
        ],
        out_specs=pl.BlockSpec((None, bsz, TM_MOD), lambda i, j: (i, 0, j)),
        out_shape=jax.ShapeDtypeStruct((depth, bsz, n), F32),
        compiler_params=_params("arbitrary", "arbitrary"),
        name="adaln_modulation",
    )(c, mod_w, mod_b.reshape(depth, 1, n))
    return out.reshape(depth, bsz, N_MOD, d)


def _ffn_kernel(*refs, sub, d_ff, final, mixed):
    x_ref, mod_ref, g_ref, w13_ref, w2_ref = refs[:5]
    o_ref, h_ref, acc_ref = refs[-3:]
    extra = refs[5:-3]
    shift = mod_ref[3 * sub:3 * sub + 1, :]
    scale = mod_ref[3 * sub + 1:3 * sub + 2, :]
    gate = mod_ref[3 * sub + 2:3 * sub + 3, :]
    rows = x_ref.shape[0] // FFN_SUBTILES

    def residual_in(i):
        x = x_ref[i * rows:(i + 1) * rows, :]
        if mixed:
            y_ref, wo_ref = extra[:2]
            x = x + mod_ref[5:6, :] * _dot(y_ref[i * rows:(i + 1) * rows, :], wo_ref[...])
        return x

    def chunk(i, c0):
        h = h_ref[i]
        a = _dot(h, w13_ref[:, c0:c0 + FFN_CHUNK])
        b = _dot(h, w13_ref[:, d_ff + c0:d_ff + c0 + FFN_CHUNK])
        part = _dot((_silu(a) * b).astype(BF16), w2_ref[c0:c0 + FFN_CHUNK, :])
        if c0 == 0:
            acc_ref[i] = part
        else:
            acc_ref[i] += part

    def finish(i, x):
        y = x + (FFN_RES * gate) * acc_ref[i]
        if final:
            fg_ref = extra[-1]
            ms = jnp.mean(y * y, axis=-1, keepdims=True)
            y = y * lax.rsqrt(ms + RMS_EPS) * fg_ref[...]
        o_ref[i * rows:(i + 1) * rows, :] = y

    starts = list(range(0, d_ff, FFN_CHUNK))
    xs = []
    for i in range(FFN_SUBTILES):
        xs.append(residual_in(i))
        h_ref[i] = _norm_mod(xs[i], g_ref[...], shift, scale).astype(BF16)
        chunk(i, starts[0])
    for c0 in starts[1:-1]:
        for i in range(FFN_SUBTILES):
            chunk(i, c0)
    for i in range(FFN_SUBTILES):
        chunk(i, starts[-1])
        finish(i, xs[i])


def _ffn(x, modl, g, w13, w2, layer, sub, mixer=None, final_g=None):
    bsz, seq, d = x.shape
    d_ff = w2.shape[-2]
    assert d_ff % FFN_CHUNK == 0
    which = sub // 2

    def weight_spec(rows, cols):
        return pl.BlockSpec((None, None, rows, cols), lambda b, l: (layer, which, 0, 0),
                            pipeline_mode=pl.Buffered(1))

    in_specs = [
        pl.BlockSpec((None, TM_FFN, d), lambda b, l: (b, l, 0)),
        pl.BlockSpec((None, N_MOD, d), lambda b, l: (b, 0, 0)),
        _const_spec((1, d)),
        weight_spec(d, 2 * d_ff),
        weight_spec(d_ff, d),
    ]
    args = [x, modl, g.reshape(1, d), w13, w2]
    if mixer is not None:
        y, w_out = mixer
        k = y.shape[-1]
        in_specs += [pl.BlockSpec((None, TM_FFN, k), lambda b, l: (b, l, 0)), _const_spec((k, d))]
        args += [y, w_out.astype(BF16)]
    if final_g is not None:
        in_specs.append(_const_spec((1, d)))
        args.append(final_g.reshape(1, d))
    return pl.pallas_call(
        functools.partial(_ffn_kernel, sub=sub, d_ff=d_ff, final=final_g is not None,
                          mixed=mixer is not None),
        grid=(bsz, seq // TM_FFN),
        in_specs=in_specs,
        out_specs=pl.BlockSpec((None, TM_FFN, d), lambda b, l: (b, l, 0)),
        out_shape=jax.ShapeDtypeStruct((bsz, seq, d), F32),
        scratch_shapes=[pltpu.VMEM((FFN_SUBTILES, TM_FFN // FFN_SUBTILES, d), BF16),
                        pltpu.VMEM((FFN_SUBTILES, TM_FFN // FFN_SUBTILES, d), F32)],
        compiler_params=_params("arbitrary", "arbitrary"),
        name="swiglu_ffn",
    )(*args)


def _ab_in_kernel(x_ref, mod_ref, g_ref, w_ref, u_ref, z_ref, xbc_ref, dt_ref, *, widths):
    rows = x_ref.shape[0] // PROJ_SUBTILES
    ps = []
    for i in range(PROJ_SUBTILES):
        r = slice(i * rows, (i + 1) * rows)
        h = _norm_mod(x_ref[r, :], g_ref[...], mod_ref[3:4, :], mod_ref[4:5, :]).astype(BF16)
        ps.append(_dot(h, w_ref[...]))
    for i, p in enumerate(ps):
        r = slice(i * rows, (i + 1) * rows)
        c0 = 0
        for ref, wd in zip((u_ref, z_ref, xbc_ref, dt_ref), widths):
            ref[r, :] = p[:, c0:c0 + wd]
            c0 += wd


def _ab_in_proj(x, modl, g, w_in, pool_dim, d_ssm, conv_dim, n_heads):
    bsz, seq, d = x.shape
    main = pool_dim + d_ssm + conv_dim
    w_dt = jnp.pad(w_in[:, main:main + n_heads], ((0, 0), (0, LANES - n_heads)))
    w = jnp.concatenate([w_in[:, :main], w_dt], axis=1).astype(BF16)
    widths = (pool_dim, d_ssm, conv_dim, LANES)
    return pl.pallas_call(
        functools.partial(_ab_in_kernel, widths=widths),
        grid=(bsz, seq // TM_PROJ),
        in_specs=[
            pl.BlockSpec((None, TM_PROJ, d), lambda b, l: (b, l, 0)),
            pl.BlockSpec((None, N_MOD, d), lambda b, l: (b, 0, 0)),
            _const_spec((1, d)),
            _const_spec(w.shape),
        ],
        out_specs=[pl.BlockSpec((None, TM_PROJ, wd), lambda b, l: (b, l, 0)) for wd in widths],
        out_shape=[jax.ShapeDtypeStruct((bsz, seq, wd), F32) for wd in widths],
        compiler_params=_params("arbitrary", "arbitrary"),
        name="pool_ssd_in_proj",
    )(x, modl, g.reshape(1, d), w)


def _softplus(x):
    return jnp.maximum(x, 0.0) + jnp.log1p(jnp.exp(-jnp.abs(x)))


def _mixer_kernel(u_ref, z_ref, xbc_ref, dtr_ref, pw_ref, ps_ref, cw_ref, cb_ref, dtb_ref,
                  aneg_ref, dsk_ref, ng_ref, rexp_ref, o_ref, xext, uext, st_ref, xc_ref,
                  *, pool_dim, d_ssm):
    tm = u_ref.shape[0]
    q = SSD_CHUNK
    n_groups = SSD_GROUPS
    gdim = d_ssm // n_groups
    heads_per_group = gdim // SSD_HEADDIM
    pool_gdim = pool_dim // len(POOL_WINDOWS)
    max_win = max(POOL_WINDOWS)
    lt = pl.program_id(1)

    @pl.when(lt == 0)
    def _():
        xext[0:SUBLANES, :] = jnp.zeros((SUBLANES, xext.shape[1]), F32)
        uext[0:max_win, :] = jnp.zeros((max_win, uext.shape[1]), F32)
        st_ref[...] = jnp.zeros_like(st_ref)

    xext[SUBLANES:SUBLANES + tm, :] = xbc_ref[...]
    ext = xext[...]
    acc = cb_ref[...] + cw_ref[CONV_K - 1:CONV_K, :] * xbc_ref[...]
    for k in range(CONV_K - 1):
        back = CONV_K - 1 - k
        acc = acc + cw_ref[k:k + 1, :] * pltpu.roll(ext, back, 0)[SUBLANES:, :]
    xc_ref[...] = _silu(acc)
    xext[0:SUBLANES, :] = xext[tm:tm + SUBLANES, :]

    uext[max_win:max_win + tm, :] = u_ref[...]
    tpos = (lt * tm + lax.broadcasted_iota(jnp.int32, (tm, pool_gdim), 0)).astype(F32)
    for gi, win in enumerate(POOL_WINDOWS):
        cols = slice(gi * pool_gdim, (gi + 1) * pool_gdim)
        cur = u_ref[:, cols]
        assert win & (win - 1) == 0 and win <= max_win
        ws = uext[:, cols]
        span = 1
        while span < win:
            ws = ws + pltpu.roll(ws, span, 0)
            span *= 2
        ws = ws[max_win:, :]
        count = jnp.minimum(tpos + 1.0, float(win))
        dlt = ws / count - cur
        yp = _dot(dlt.astype(BF16), pw_ref[gi]) * ps_ref[:, cols]
        o_ref[:, cols] = yp.astype(BF16)
    uext[0:max_win, :] = uext[tm:tm + max_win, :]

    row = lax.broadcasted_iota(jnp.int32, (q, q), 0)
    col = lax.broadcasted_iota(jnp.int32, (q, q), 1)
    causal = row >= col
    tril = jnp.where(causal, 1.0, 0.0).astype(BF16)
    lane = lax.broadcasted_iota(jnp.int32, (1, LANES), 1)
    lo_half = lane < SSD_HEADDIM
    rexp = rexp_ref[...]

    def expand(v):
        hi, lo = _split_bf16(v, 2)
        return _dot(hi, rexp) + _dot(lo, rexp)

    for ci in range(tm // q):
        r0 = ci * q
        xs = xc_ref[r0:r0 + q, 0:d_ssm]
        dt = _softplus(dtr_ref[r0:r0 + q, :] + dtb_ref[...])
        adt = dt * aneg_ref[...]
        acs = sum(_dot(tril, p) for p in _split_bf16(adt, 3))
        acs_t = acs.T
        a_last = acs[q - 1:q, :]
        dt_e = expand(dt)
        eacs_e = expand(jnp.exp(acs))
        wend_e = expand(jnp.exp(a_last - acs) * dt)
        xdt = xs * dt_e
        xw = xs * wend_e
        for g in range(n_groups):
            gc = slice(g * gdim, (g + 1) * gdim)
            b0 = d_ssm + g * D_STATE
            c0 = d_ssm + n_groups * D_STATE + g * D_STATE
            bm = xc_ref[r0:r0 + q, b0:b0 + D_STATE]
            cm = xc_ref[r0:r0 + q, c0:c0 + D_STATE].astype(BF16)
            cb = _dot_nt(cm, bm.astype(BF16))
            s_prev = st_ref[g]
            y_off = _dot(cm, s_prev.astype(BF16)) * eacs_e[:, gc]
            ys = []
            for pr in range(heads_per_group // 2):
                h0 = g * heads_per_group + 2 * pr
                gs = []
                for hh in range(2):
                    h = h0 + hh
                    diff = acs[:, h:h + 1] - acs_t[h:h + 1, :]
                    dec = jnp.exp(jnp.where(causal, diff, -jnp.inf))
                    gs.append((cb * dec).astype(BF16))
                xp = xdt[:, h0 * SSD_HEADDIM:(h0 + 2) * SSD_HEADDIM]
                xbd = jnp.concatenate([jnp.where(lo_half, xp, 0.0), jnp.where(lo_half, 0.0, xp)],
                                      axis=0).astype(BF16)
                ys.append(_dot(jnp.concatenate(gs, axis=1), xbd))
            y = jnp.concatenate(ys, axis=1) + y_off
            st_ref[g] = eacs_e[q - 1:q, gc] * s_prev + _dot(bm.T.astype(BF16), xw[:, gc].astype(BF16))
            y = y + dsk_ref[:, gc] * xs[:, gc]
            y = y * _silu(z_ref[r0:r0 + q, gc])
            ms = jnp.mean(y * y, axis=-1, keepdims=True)
            y = y * lax.rsqrt(ms + RMS_EPS) * ng_ref[:, gc]
            o_ref[r0:r0 + q, pool_dim + g * gdim:pool_dim + (g + 1) * gdim] = y.astype(BF16)


def _mixer_core(u, z, xbc, dtr, pool_w, pool_scale, conv_w, conv_b, dt_bias, a_log, d_skip, norm_g):
    bsz, seq, pool_dim = u.shape
    d_ssm = z.shape[-1]
    conv_dim = xbc.shape[-1]
    n_heads = dt_bias.shape[0]
    pad = LANES - n_heads
    dtb = jnp.pad(dt_bias.astype(F32), (0, pad)).reshape(1, LANES)
    aneg = jnp.pad(-jnp.exp(a_log.astype(F32)), (0, pad)).reshape(1, LANES)
    dsk = jnp.repeat(d_skip.astype(F32), SSD_HEADDIM).reshape(1, d_ssm)
    rexp = (jnp.arange(LANES)[:, None] == (jnp.arange(d_ssm) // SSD_HEADDIM)[None, :]).astype(BF16)
    consts = [pool_w.astype(BF16), pool_scale.reshape(1, pool_dim), conv_w, conv_b.reshape(1, conv_dim),
              dtb, aneg, dsk, norm_g.reshape(1, d_ssm), rexp]
    widths = (pool_dim, d_ssm, conv_dim, LANES)
    max_win = max(POOL_WINDOWS)
    return pl.pallas_call(
        functools.partial(_mixer_kernel, pool_dim=pool_dim, d_ssm=d_ssm),
        grid=(bsz, seq // TM_MIX),
        in_specs=[pl.BlockSpec((None, TM_MIX, wd), lambda b, l: (b, l, 0)) for wd in widths]
        + [_const_spec(a.shape) for a in consts],
        out_specs=pl.BlockSpec((None, TM_MIX, pool_dim + d_ssm), lambda b, l: (b, l, 0)),
        out_shape=jax.ShapeDtypeStruct((bsz, seq, pool_dim + d_ssm), BF16),
        scratch_shapes=[
            pltpu.VMEM((SUBLANES + TM_MIX, conv_dim), F32),
            pltpu.VMEM((max_win + TM_MIX, pool_dim), F32),
            pltpu.VMEM((SSD_GROUPS, D_STATE, d_ssm // SSD_GROUPS), F32),
            pltpu.VMEM((TM_MIX, conv_dim), F32),
        ],
        compiler_params=_params("arbitrary", "arbitrary"),
        name="pool_ssd_core",
    )(u, z, xbc, dtr, *consts)


def _mla_proj_kernel(x_ref, mod_ref, g_ref, pos_ref, win_ref, qg_ref, wuq_ref, kvg_ref, wk_ref,
                     wv_ref, invf_ref, q_ref, k_ref, v_ref, *, q_lora, kv_lora, scale):
    rows = x_ref.shape[0] // PROJ_SUBTILES
    n_heads = q_ref.shape[0]
    lane = lax.broadcasted_iota(jnp.int32, (1, HEAD_PAD), 1)
    first = (lane >= ROPE_LANE0) & (lane < ROPE_LANE0 + HALF_ROPE)
    second = (lane >= ROPE_LANE0 + HALF_ROPE) & (lane < ROPE_LANE0 + QK_ROPE)

    def rms(t, gain):
        return t * lax.rsqrt(jnp.mean(t * t, axis=-1, keepdims=True) + RMS_EPS) * gain

    def in_proj(i):
        r = slice(i * rows, (i + 1) * rows)
        h = _norm_mod(x_ref[r, :], g_ref[...], mod_ref[3:4, :], mod_ref[4:5, :]).astype(BF16)
        return _dot(h, win_ref[...])

    def latents(i, p):
        r = slice(i * rows, (i + 1) * rows)
        qn = rms(p[:, 0:q_lora], qg_ref[...]).astype(BF16)
        kvn = rms(p[:, q_lora:q_lora + kv_lora], kvg_ref[...]).astype(BF16)
        k_r = p[:, q_lora + kv_lora:q_lora + kv_lora + HEAD_PAD]
        ang = pos_ref[r, :] * invf_ref[...]
        sn = jnp.sin(ang)
        return qn, kvn, k_r, (jnp.cos(ang), jnp.where(first, -sn, 0.0), jnp.where(second, sn, 0.0))

    def up_proj(qn, kvn):
        return _dot(qn, wuq_ref[...]), _dot(kvn, wk_ref[...]), _dot_nt(wv_ref[...], kvn)

    def rope(t, tables, mult):
        cs, s_first, s_second = tables
        up = pltpu.roll(t, HEAD_PAD - HALF_ROPE, 1)
        down = pltpu.roll(t, HALF_ROPE, 1)
        return (t * cs + up * s_first + down * s_second) * mult

    def write(i, qf, kf, vt, k_r, tables):
        r = slice(i * rows, (i + 1) * rows)
        k_rot = rope(k_r, tables, 1.0)
        for hd in range(n_heads):
            cols = slice(hd * HEAD_PAD, (hd + 1) * HEAD_PAD)
            q_ref[hd, r, :] = rope(qf[:, cols], tables, scale).astype(BF16)
            k_ref[hd, r, :] = (kf[:, cols] + k_rot).astype(BF16)
        for hp in range(v_ref.shape[0]):
            v_ref[hp, :, r] = vt[hp * LANES:(hp + 1) * LANES, :].astype(BF16)

    ps = [in_proj(i) for i in range(PROJ_SUBTILES)]
    pending = None
    for i in range(PROJ_SUBTILES):
        qn, kvn, k_r, tables = latents(i, ps[i])
        ups = up_proj(qn, kvn)
        if pending is not None:
            write(*pending)
        pending = (i, *ups, k_r, tables)
    write(*pending)


def _mla_proj(x, modl, g, positions, w_in, q_norm_g, w_uq, kv_norm_g, w_ukv):
    bsz, seq, d = x.shape
    q_lora = w_uq.shape[0]
    kv_lora = w_ukv.shape[0]
    nh = MLA_HEADS
    w_rope = jnp.pad(w_in[:, q_lora + kv_lora:], ((0, 0), (ROPE_LANE0, HEAD_PAD - ROPE_LANE0 - QK_ROPE)))
    win = jnp.concatenate([w_in[:, :q_lora + kv_lora], w_rope], axis=1).astype(BF16)
    wuq = jnp.pad(w_uq.reshape(q_lora, nh, QK_NOPE + QK_ROPE),
                  ((0, 0), (0, 0), (0, HEAD_PAD - QK_NOPE - QK_ROPE))).reshape(q_lora, nh * HEAD_PAD)
    wkv = w_ukv.reshape(kv_lora, nh, QK_NOPE + V_DIM)
    wk = jnp.pad(wkv[:, :, :QK_NOPE], ((0, 0), (0, 0), (0, HEAD_PAD - QK_NOPE))).reshape(kv_lora, nh * HEAD_PAD)
    wv = wkv[:, :, QK_NOPE:].reshape(kv_lora, nh * V_DIM).T
    inv_freq = ROPE_THETA ** (-jnp.arange(0, QK_ROPE, 2, dtype=F32) / QK_ROPE)
    invf = jnp.zeros((HEAD_PAD,), F32).at[ROPE_LANE0:ROPE_LANE0 + QK_ROPE].set(jnp.tile(inv_freq, 2))
    pos = positions.astype(F32).reshape(bsz, seq, 1)
    scale = math.log2(math.e) / math.sqrt(QK_NOPE + QK_ROPE)
    consts = [win, q_norm_g.reshape(1, q_lora), wuq.astype(BF16), kv_norm_g.reshape(1, kv_lora),
              wk.astype(BF16), wv.astype(BF16), invf.reshape(1, HEAD_PAD)]
    n_vp = nh * V_DIM // LANES
    return pl.pallas_call(
        functools.partial(_mla_proj_kernel, q_lora=q_lora, kv_lora=kv_lora, scale=scale),
        grid=(bsz, seq // TM_PROJ),
        in_specs=[
            pl.BlockSpec((None, TM_PROJ, d), lambda b, l: (b, l, 0)),
            pl.BlockSpec((None, N_MOD, d), lambda b, l: (b, 0, 0)),
            _const_spec((1, d)),
            pl.BlockSpec((None, TM_PROJ, 1), lambda b, l: (b, l, 0)),
        ] + [_const_spec(a.shape) for a in consts],
        out_specs=[
            pl.BlockSpec((None, nh, TM_PROJ, HEAD_PAD), lambda b, l: (b, 0, l, 0)),
            pl.BlockSpec((None, nh, TM_PROJ, HEAD_PAD), lambda b, l: (b, 0, l, 0)),
            pl.BlockSpec((None, n_vp, LANES, TM_PROJ), lambda b, l: (b, 0, 0, l)),
        ],
        out_shape=[
            jax.ShapeDtypeStruct((bsz, nh, seq, HEAD_PAD), BF16),
            jax.ShapeDtypeStruct((bsz, nh, seq, HEAD_PAD), BF16),
            jax.ShapeDtypeStruct((bsz, n_vp, LANES, seq), BF16),
        ],
        compiler_params=_params("arbitrary", "arbitrary"),
        name="mla_qkv_proj",
    )(x, modl, g.reshape(1, d), pos, *consts)


def _attn_kernel(q_ref, k_ref, vt_ref, o_ref, sa_ref, sb_ref, pa_ref, pb_ref):
    seq = vt_ref.shape[2]
    n_heads = q_ref.shape[0]
    blocks = TQ // TK
    assert TQ % (2 * TK) == 0
    n_groups = TQ // LANES
    block_groups = TK // LANES
    row_minus_lane = (lax.broadcasted_iota(jnp.int32, (TK, LANES), 0)
                      - lax.broadcasted_iota(jnp.int32, (TK, LANES), 1))
    ones_rows = jnp.ones((BF16_SUBLANES, TK), BF16)

    def load_q(q0):
        return [q_ref[hh, pl.ds(q0, TQ), :] for hh in range(n_heads)]

    def scores(dst_ref, qs, k0, lo=0):
        for hh in range(n_heads):
            dst_ref[hh, :, lo:] = _dot_nt(k_ref[hh, pl.ds(k0, TK), :], qs[hh][lo:])

    def softmax(state, s_ref, p_ref, groups):
        out = []
        for hh in range(n_heads):
            m_old = state[3 * hh]
            ms, alphas = [], []
            for grp, delta in groups:
                cols = slice(grp * LANES, (grp + 1) * LANES)
                st = s_ref[hh, :, cols]
                if delta is not None:
                    st = jnp.where(row_minus_lane <= delta, st, -jnp.inf)
                m_prev = m_old[:, cols]
                m_new = jnp.maximum(m_prev, jnp.max(st, axis=0, keepdims=True))
                ms.append(m_new)
                alphas.append(jnp.exp2(m_prev - m_new))
                p_ref[hh, :, cols] = jnp.exp2(st - m_new).astype(BF16)
            out.append((jnp.concatenate(ms, axis=1), jnp.concatenate(alphas, axis=1)))
        return out

    def accumulate(state, soft, p_ref, k0, lo):
        vts = [vt_ref[hp, :, pl.ds(k0, TK)] for hp in range(n_heads // 2)]
        new = []
        for hh in range(n_heads):
            m_old, l_old, acc = state[3 * hh:3 * hh + 3]
            m_new, alpha = soft[hh]
            v_rows = vts[hh // 2][(hh % 2) * V_DIM:(hh % 2 + 1) * V_DIM, :]
            lhs = jnp.concatenate([v_rows, ones_rows], axis=0)
            pv = _dot(lhs, p_ref[hh, :, lo:])
            upd = [m_new, alpha * l_old[:, lo:] + pv[V_DIM:V_DIM + 1, :], acc[:, lo:] * alpha + pv[:V_DIM, :]]
            if lo:
                upd = [jnp.concatenate([old[:, :lo], u], axis=1) for old, u in zip((m_old, l_old, acc), upd)]
            new += upd
        return tuple(new)

    below = [(grp, None) for grp in range(n_groups)]

    def diagonal(d):
        first = d * block_groups
        return [(grp, (grp - first) * LANES if grp < first + block_groups else None)
                for grp in range(first, n_groups)]

    def q_tile(qi, carry):
        q0 = pl.multiple_of(qi * TQ, TQ)
        qs = load_q(q0)

        def pair(t, state):
            k0 = pl.multiple_of(2 * t * TK, 2 * TK)
            soft = softmax(state, sa_ref, pa_ref, below)
            scores(sa_ref, qs, k0 + 2 * TK)
            state = accumulate(state, soft, pa_ref, k0, 0)
            soft = softmax(state, sb_ref, pb_ref, below)
            scores(sb_ref, qs, k0 + 3 * TK)
            return accumulate(state, soft, pb_ref, k0 + TK, 0)

        init = (jnp.full((1, TQ), -jnp.inf, F32), jnp.zeros((1, TQ), F32),
                jnp.zeros((V_DIM, TQ), F32)) * n_heads
        state = lax.fori_loop(0, qi * (blocks // 2), pair, init)
        next_qs = load_q(pl.multiple_of(jnp.minimum(q0 + TQ, seq - TQ), TQ))
        for d in range(blocks):
            s_ref, p_ref = (sa_ref, pa_ref) if d % 2 == 0 else (sb_ref, pb_ref)
            soft = softmax(state, s_ref, p_ref, diagonal(d))
            if d + 2 < blocks:
                scores(s_ref, qs, q0 + (d + 2) * TK, (d + 2) * TK)
            else:
                scores(s_ref, next_qs, (d + 2 - blocks) * TK)
            state = accumulate(state, soft, p_ref, q0 + d * TK, d * TK)
        out_t = jnp.concatenate([state[3 * hh + 2] / state[3 * hh + 1] for hh in range(n_heads)], axis=0)
        o_ref[pl.ds(q0, TQ), :] = out_t.T.astype(BF16)
        return carry

    first_qs = load_q(0)
    scores(sa_ref, first_qs, 0)
    scores(sb_ref, first_qs, TK)
    lax.fori_loop(0, seq // TQ, q_tile, 0)


def _attention(q, k, v):
    bsz, nh, seq, _ = q.shape
    return pl.pallas_call(
        _attn_kernel,
        grid=(bsz, nh // ATTN_HEADS),
        in_specs=[
            pl.BlockSpec((None, ATTN_HEADS, seq, HEAD_PAD), lambda b, hg: (b, hg, 0, 0)),
            pl.BlockSpec((None, ATTN_HEADS, seq, HEAD_PAD), lambda b, hg: (b, hg, 0, 0)),
            pl.BlockSpec((None, ATTN_HEADS // 2, LANES, seq), lambda b, hg: (b, hg, 0, 0)),
        ],
        out_specs=pl.BlockSpec((None, seq, ATTN_HEADS * V_DIM), lambda b, hg: (b, 0, hg)),
        out_shape=jax.ShapeDtypeStruct((bsz, seq, nh * V_DIM), BF16),
        scratch_shapes=([pltpu.VMEM((ATTN_HEADS, TK, TQ), F32)] * 2
                        + [pltpu.VMEM((ATTN_HEADS, TK, TQ), BF16)] * 2),
        compiler_params=_params("arbitrary", "arbitrary"),
        name="mla_causal_attention",
    )(q, k, v)


def kernel(x, c, positions, mod_w, mod_b, norm_g, ffn_w13, ffn_w2, ab_w_in, pool_w, pool_scale, ssd_conv_w, ssd_conv_b, ssd_dt_bias, ssd_a_log, ssd_d, ssd_norm_g, ab_w_out, mla_w_in, mla_q_norm_g, mla_w_uq, mla_kv_norm_g, mla_w_ukv, mla_w_o, final_norm_g):
    depth = mod_w.shape[0]
    mod = _modulation(c, mod_w, mod_b)
    w13, w2 = ffn_w13.astype(BF16), ffn_w2.astype(BF16)
    for i in range(depth):
        modl = mod[i]
        j = i // 2
        x = _ffn(x, modl, norm_g[i, 0], w13, w2, i, sub=0)
        if i % 2 == 0:
            pool_dim = pool_scale.shape[-1]
            d_ssm = ssd_norm_g.shape[-1]
            conv_dim = ssd_conv_b.shape[-1]
            n_heads = ssd_dt_bias.shape[-1]
            u, z, xbc, dtr = _ab_in_proj(x, modl, norm_g[i, 1], ab_w_in[j], pool_dim, d_ssm, conv_dim, n_heads)
            y = _mixer_core(u, z, xbc, dtr, pool_w[j], pool_scale[j], ssd_conv_w[j], ssd_conv_b[j],
                            ssd_dt_bias[j], ssd_a_log[j], ssd_d[j], ssd_norm_g[j])
            mixer = (y, ab_w_out[j])
        else:
            q, k, v = _mla_proj(x, modl, norm_g[i, 1], positions, mla_w_in[j], mla_q_norm_g[j],
                                mla_w_uq[j], mla_kv_norm_g[j], mla_w_ukv[j])
            mixer = (_attention(q, k, v), mla_w_o[j])
        last = i == depth - 1
        x = _ffn(x, modl, norm_g[i, 2], w13, w2, i, sub=2, mixer=mixer,
                 final_g=final_norm_g if last else None)
    return x
```
